```python
import math
import jax, jax.numpy as jnp
from jax import lax
import numpy as np

D_MODEL = 1024
BATCH = 32
SEQ = 256
DEPTH = 2
DEC_BATCH = 4
DEC_SEQ = 4096
PAST_LEN = 256

GRID_W = 64
D_A = D_MODEL
HEAD_A = 64
H_A = D_A // HEAD_A
N_DIR = 2
W_LORA = 64
A_LORA = 64
G_LORA = 128
D_B = D_MODEL
CHUNK = 128
H_B = 8
HEAD_B = D_B // H_B
D_FF = 4 * D_MODEL
C_RWKV = 3 * D_A + N_DIR * (W_LORA + A_LORA) + G_LORA
D_IN = C_RWKV + 2 * D_B + 2 * D_MODEL
N_MOD = 6
EPS = 1e-6
GN_EPS = 64e-5
DECAY_SCALE = math.exp(-0.5)

kernel_name = 'hybrid_rwkv7_gmlp_diffusion_step'


def rmsnorm(x, g):
    xf = x.astype(jnp.float32)
    y = xf * lax.rsqrt(jnp.mean(xf * xf, axis=-1, keepdims=True) + EPS)
    return (y * g).astype(x.dtype)


def seq_shift(z):
    B, T, C = z.shape
    g = z.reshape(B, T, C // 2, 2)
    prev = jnp.pad(g[:, :-1, :, 0], ((0, 0), (1, 0), (0, 0)))
    nxt = jnp.pad(g[:, 1:, :, 1], ((0, 0), (0, 1), (0, 0)))
    return jnp.stack([prev, nxt], axis=-1).reshape(B, T, C)


def grid_shift(z):
    B, T, C = z.shape
    rows = T // GRID_W
    g = z.reshape(B, rows, GRID_W, C // 4, 4)
    left = jnp.pad(g[:, :, :-1, :, 0], ((0, 0), (0, 0), (1, 0), (0, 0)))
    right = jnp.pad(g[:, :, 1:, :, 1], ((0, 0), (0, 0), (0, 1), (0, 0)))
    up = jnp.pad(g[:, :-1, :, :, 2], ((0, 0), (1, 0), (0, 0), (0, 0)))
    down = jnp.pad(g[:, 1:, :, :, 3], ((0, 0), (0, 1), (0, 0), (0, 0)))
    return jnp.stack([left, right, up, down], axis=-1).reshape(B, T, C)


def _heads(t):
    return t.reshape(t.shape[:-1] + (H_A, HEAD_A))


def _bi_shared(t):
    return jnp.stack([t, jnp.flip(t, axis=1)])


def _bi_dir(t):
    return jnp.stack([t[0], jnp.flip(t[1], axis=1)])


def _rwkv_step(S, inp):
    r, w, k, v, neg_kk, b = inp
    sa = jnp.einsum('dbhij,dbhj->dbhi', S, neg_kk)
    S = S * w[..., None, :] + sa[..., :, None] * b[..., None, :] + v[..., :, None] * k[..., None, :]
    return S, jnp.einsum('dbhij,dbhj->dbhi', S, r)


def rwkv_branch(z, S0, p, l):
    B, T, _ = z.shape
    f32 = jnp.float32
    r, k, v, wd, ad, gd = jnp.split(
        z, [D_A, 2 * D_A, 3 * D_A, 3 * D_A + N_DIR * W_LORA, 3 * D_A + N_DIR * (W_LORA + A_LORA)], axis=-1)
    r, k, v = r.astype(f32), k.astype(f32), v.astype(f32)
    wd = jnp.tanh(wd.reshape(B, T, N_DIR, W_LORA).astype(f32))
    w_logit = p['w0'][l][:, None, None, :] + jnp.einsum('btdr,drc->dbtc', wd, p['w_up'][l])
    decay = jnp.exp(-DECAY_SCALE * jax.nn.sigmoid(w_logit))
    a = jax.nn.sigmoid(p['a0'][l][:, None, None, :] + jnp.einsum(
        'btdr,drc->dbtc', ad.reshape(B, T, N_DIR, A_LORA).astype(f32), p['a_up'][l]))
    g = jax.nn.sigmoid(gd.astype(f32)) @ p['g_up'][l]
    kk = _heads(k * p['k_k'][l])
    kk = kk * lax.rsqrt(jnp.sum(kk * kk, axis=-1, keepdims=True) + 1e-12)
    k_d = k[None] * (1 + (a - 1) * p['k_a'][l])
    rh, vh, k_dh = _heads(r), _heads(v), _heads(k_d)
    xs = (_bi_shared(rh), _bi_dir(_heads(decay)), _bi_dir(k_dh), _bi_shared(vh),
          _bi_shared(-kk), _bi_dir(_heads(a)) * _bi_shared(kk))
    xs = tuple(jnp.moveaxis(t, 2, 0) for t in xs)
    S_fin, ys = lax.scan(_rwkv_step, S0.astype(f32), xs)
    ys = jnp.moveaxis(ys, 0, 2)
    y = ys[0] + jnp.flip(ys[1], axis=1)
    mu = jnp.mean(y, axis=-1, keepdims=True)
    var = jnp.mean(jnp.square(y - mu), axis=-1, keepdims=True)
    y = (y - mu) * lax.rsqrt(var + GN_EPS) * _heads(p['lnx_g'][l]) + _heads(p['lnx_b'][l])
    bonus = jnp.einsum('dbthn,bthn->bth', k_dh * p['r_k'][l], rh)[..., None] * vh
    y = (y + bonus).reshape(B, T, D_A) * g
    return y.astype(z.dtype) @ p['w_branch_a'][l], S_fin


def chunk_mlp_branch(zu, zv, p, l):
    B, T, _ = zu.shape
    u = jax.nn.gelu(zu)
    v = jax.nn.gelu(zv).astype(jnp.float32)
    mu = jnp.mean(v, axis=-1, keepdims=True)
    var = jnp.mean(jnp.square(v - mu), axis=-1, keepdims=True)
    v = (v - mu) * lax.rsqrt(var + EPS) * p['ln_v_g'][l]
    vc = v.reshape(B, T // CHUNK, CHUNK, H_B, HEAD_B)
    s = jnp.einsum('hpq,bnqhc->bnphc', p['w_s'][l], vc) + jnp.transpose(p['b_s'][l])[:, :, None]
    y = u * s.reshape(B, T, D_B).astype(u.dtype)
    return y @ p['w_branch_b'][l]


def trunk_layer(x, cond, S0, shift_fn, p, l):
    mod = jax.nn.silu(cond) @ p['w_ada'][l] + p['b_ada'][l]
    sh1, sc1, g1, sh2, sc2, g2 = jnp.split(mod[:, None, :], N_MOD, axis=-1)
    h = rmsnorm(x, p['norm1_g'][l]) * (1 + sc1) + sh1
    z = h @ p['w_in'][l]
    z_rwkv, z_u, z_v, z_ga, z_gb = jnp.split(
        z, [C_RWKV, C_RWKV + D_B, C_RWKV + 2 * D_B, C_RWKV + 2 * D_B + D_MODEL], axis=-1)
    z_rwkv = z_rwkv + p['mu_shift'][l] * (shift_fn(z_rwkv) - z_rwkv)
    y_a, S_fin = rwkv_branch(z_rwkv, S0, p, l)
    y_b = chunk_mlp_branch(z_u, z_v, p, l)
    mixed = (jax.nn.sigmoid(z_ga) * y_a + jax.nn.sigmoid(z_gb) * y_b) @ p['w_out'][l]
    x = x + g1 * mixed
    h2 = rmsnorm(x, p['norm2_g'][l]) * (1 + sc2) + sh2
    x = x + g2 * (jnp.square(jax.nn.relu(h2 @ p['w1'][l])) @ p['w2'][l])
    return x, S_fin


def setup_inputs(seed: int = 0) -> dict:
    key = jax.random.key(seed)
    ks = jax.random.split(key, 32)
    f32 = jnp.float32

    def nrm(k, shape, scale):
        return jax.random.normal(k, shape, f32) * scale

    return {
        'x_prompt': nrm(ks[0], (BATCH, SEQ, D_MODEL), 1.0),
        'x_sample': nrm(ks[1], (DEC_BATCH, DEC_SEQ, D_MODEL), 1.0),
        'state_rwkv': nrm(ks[2], (DEC_BATCH, DEPTH, N_DIR, H_A, HEAD_A, HEAD_A), 0.5),
        'c': nrm(ks[3], (DEC_BATCH, D_MODEL), 1.0),
        'c_ctx': nrm(ks[4], (D_MODEL,), 1.0),
        'w_ada': nrm(ks[5], (DEPTH, D_MODEL, N_MOD * D_MODEL), 0.5 * D_MODEL ** -0.5),
        'b_ada': nrm(ks[6], (DEPTH, N_MOD * D_MODEL), 0.02),
        'norm1_g': 1.0 + nrm(ks[7], (DEPTH, D_MODEL), 0.02),
        'norm2_g': 1.0 + nrm(ks[8], (DEPTH, D_MODEL), 0.02),
        'w_in': nrm(ks[9], (DEPTH, D_MODEL, D_IN), D_MODEL ** -0.5),
        'mu_shift': jax.random.uniform(ks[10], (DEPTH, C_RWKV), f32),
        'w0': nrm(ks[11], (DEPTH, N_DIR, D_A), 0.5),
        'w_up': nrm(ks[12], (DEPTH, N_DIR, W_LORA, D_A), 0.5 * W_LORA ** -0.5),
        'a0': nrm(ks[13], (DEPTH, N_DIR, D_A), 0.5),
        'a_up': nrm(ks[14], (DEPTH, N_DIR, A_LORA, D_A), 0.5 * A_LORA ** -0.5),
        'g_up': nrm(ks[15], (DEPTH, G_LORA, D_A), G_LORA ** -0.5),
        'k_k': 0.85 + nrm(ks[16], (DEPTH, D_A), 0.05),
        'k_a': 1.0 + nrm(ks[17], (DEPTH, D_A), 0.05),
        'r_k': nrm(ks[18], (DEPTH, H_A, HEAD_A), 0.1),
        'lnx_g': 1.0 + nrm(ks[19], (DEPTH, D_A), 0.02),
        'lnx_b': nrm(ks[20], (DEPTH, D_A), 0.02),
        'w_branch_a': nrm(ks[21], (DEPTH, D_A, D_MODEL), D_A ** -0.5),
        'ln_v_g': 1.0 + nrm(ks[22], (DEPTH, D_B), 0.02),
        'w_s': nrm(ks[23], (DEPTH, H_B, CHUNK, CHUNK), 0.5 * CHUNK ** -0.5),
        'b_s': 1.0 + nrm(ks[24], (DEPTH, H_B, CHUNK), 0.02),
        'w_branch_b': nrm(ks[25], (DEPTH, D_B, D_MODEL), D_B ** -0.5),
        'w_out': nrm(ks[26], (DEPTH, D_MODEL, D_MODEL), D_MODEL ** -0.5),
        'w1': nrm(ks[27], (DEPTH, D_MODEL, D_FF), D_MODEL ** -0.5),
        'w2': nrm(ks[28], (DEPTH, D_FF, D_MODEL), D_FF ** -0.5),
        'final_g': 1.0 + nrm(ks[29], (D_MODEL,), 0.02),
    }


def reference(x_prompt, x_sample, state_rwkv, c, c_ctx, w_ada, b_ada, norm1_g, norm2_g, w_in, mu_shift,
              w0, w_up, a0, a_up, g_up, k_k, k_a, r_k, lnx_g, lnx_b, w_branch_a, ln_v_g, w_s, b_s,
              w_branch_b, w_out, w1, w2, final_g):
    p = dict(w_ada=w_ada, b_ada=b_ada, norm1_g=norm1_g, norm2_g=norm2_g, w_in=w_in, mu_shift=mu_shift,
             w0=w0, w_up=w_up, a0=a0, a_up=a_up, g_up=g_up, k_k=k_k, k_a=k_a, r_k=r_k, lnx_g=lnx_g,
             lnx_b=lnx_b, w_branch_a=w_branch_a, ln_v_g=ln_v_g, w_s=w_s, b_s=b_s, w_branch_b=w_branch_b,
             w_out=w_out, w1=w1, w2=w2)
    B_ctx = x_prompt.shape[0]
    S_zero = jnp.zeros((N_DIR, B_ctx, H_A, HEAD_A, HEAD_A), jnp.float32)
    xp, xs = x_prompt, x_sample
    ctx_states = []
    for l in range(DEPTH):
        xp, S_ctx = trunk_layer(xp, c_ctx[None, :], S_zero, seq_shift, p, l)
        ctx_states.append(jnp.moveaxis(S_ctx, 0, 1))
        S_lat0 = jnp.moveaxis(state_rwkv[:, l], 1, 0)
        xs, _ = trunk_layer(xs, c, S_lat0, grid_shift, p, l)
    y_prompt = rmsnorm(xp, final_g)
    y_sample = rmsnorm(xs, final_g)
    new_state_rwkv = jnp.stack(ctx_states, axis=1)
    return (y_prompt, y_sample, new_state_rwkv)
```

```python
import functools
import math

import numpy as np
import jax
import jax.numpy as jnp
from jax import lax
from jax.experimental import pallas as pl
from jax.experimental.pallas import tpu as pltpu

F32 = jnp.float32
BF16 = jnp.bfloat16

D_MODEL = 1024
HEAD = 64
N_HEAD = D_MODEL // HEAD
LORA = 64
G_LORA = 128
C_RWKV = 3 * D_MODEL + 4 * LORA + G_LORA
D_REST = 4 * D_MODEL
D_FF = 4 * D_MODEL
GRID_W = 64
CHUNK = 128
H_B = 8
HEAD_B = D_MODEL // H_B
N_MOD = 6
EPS = 1e-6
GN_EPS = 64e-5
DECAY_SCALE = math.exp(-0.5)

SCAN_L = 64
LANE = 128
TM_TOK = 256
VMEM_LIMIT = 56 * 1024 * 1024


def _cparams(sem):
    return pltpu.CompilerParams(dimension_semantics=sem, vmem_limit_bytes=VMEM_LIMIT)


def _split(x):
    hi = x.astype(BF16)
    lo = (x - hi.astype(F32)).astype(BF16)
    return hi, lo


def _dot(a, b, dims=(((1,), (0,)), ((), ()))):
    return lax.dot_general(a, b, dims, preferred_element_type=F32)


_NN = (((1,), (0,)), ((), ()))
_NT = (((1,), (1,)), ((), ()))
_TN = (((0,), (0,)), ((), ()))


def _dot3(a, b, dims=_NN):
    ah, al = _split(a)
    bh, bl = _split(b)
    return _dot(ah, bh, dims) + (_dot(ah, bl, dims) + _dot(al, bh, dims))


def _seg_sum(x, e_ref, et_ref):
    hi, lo = _split(x)
    s = _dot(hi, e_ref[...]) + _dot(lo, e_ref[...])
    shi, slo = _split(s)
    return _dot(shi, et_ref[...]) + _dot(slo, et_ref[...])


def _mod_kernel(c_ref, w_ref, b_ref, o_ref):
    cond = c_ref[...]
    o_ref[0] = _dot(jax.nn.silu(cond).astype(BF16), w_ref[0].astype(BF16)) + b_ref[0]


def _modulation(cond, w_ada, b_ada):
    depth = w_ada.shape[0]
    n = cond.shape[0]
    tn = 1536
    return pl.pallas_call(
        _mod_kernel,
        grid=(depth, (N_MOD * D_MODEL) // tn),
        in_specs=[pl.BlockSpec((n, D_MODEL), lambda l, j: (0, 0)),
                  pl.BlockSpec((1, D_MODEL, tn), lambda l, j: (l, 0, j)),
                  pl.BlockSpec((1, 1, tn), lambda l, j: (l, 0, j))],
        out_specs=pl.BlockSpec((1, n, tn), lambda l, j: (l, 0, j)),
        out_shape=jax.ShapeDtypeStruct((depth, n, N_MOD * D_MODEL), F32),
        compiler_params=_cparams(("parallel", "parallel")),
        name="modulation",
    )(cond, w_ada, b_ada.reshape(depth, 1, N_MOD * D_MODEL))


def _mod_row_map(n_ctx_tok, t_lat, tm):
    def row(i):
        tok = i * tm
        return jnp.where(tok < n_ctx_tok, 0, 1 + (tok - n_ctx_tok) // t_lat)
    return row


def _rms_mod(x, g, shift, scale):
    y = x * lax.rsqrt(jnp.mean(x * x, axis=-1, keepdims=True) + EPS) * g
    return y * (1.0 + scale) + shift


def _in_proj_kernel(x_ref, mod_ref, g_ref, w_ref, o_ref, h_scr):
    @pl.when(pl.program_id(1) == 0)
    def _():
        m = mod_ref[0]
        h_scr[...] = _rms_mod(x_ref[...], g_ref[...], m[0:1], m[1:2]).astype(BF16)

    o_ref[...] = _dot(h_scr[...], w_ref[...])


def _in_proj(x, mod_l, g, w, tn, mod_row, tm=512):
    n_tok = x.shape[0]
    n_out = w.shape[1]
    return pl.pallas_call(
        _in_proj_kernel,
        grid=(n_tok // tm, n_out // tn),
        in_specs=[pl.BlockSpec((tm, D_MODEL), lambda i, j: (i, 0)),
                  pl.BlockSpec((1, N_MOD, D_MODEL), lambda i, j: (mod_row(i), 0, 0)),
                  pl.BlockSpec((1, D_MODEL), lambda i, j: (0, 0)),
                  pl.BlockSpec((D_MODEL, tn), lambda i, j: (0, j))],
        out_specs=pl.BlockSpec((tm, tn), lambda i, j: (i, j)),
        out_shape=jax.ShapeDtypeStruct((n_tok, n_out), F32),
        scratch_shapes=[pltpu.VMEM((tm, D_MODEL), BF16)],
        compiler_params=_cparams(("parallel", "arbitrary")),
        name="in_proj",
    )(x, mod_l, g, w)


def _shift_rows(z, k, fill_first):
    return jnp.concatenate([fill_first, z[: z.shape[0] - k]], axis=0)


def _prep_kernel(z_ref, zp_ref, zn_ref, mu_ref, wup_ref, aup_ref, gup_ref, w0_ref, a0_ref, kk_ref, ka_ref,
                 rk_ref, e_ref, et_ref,
                 r_ref, v_ref, nk_ref, g_ref, bv_ref, lw_ref, kd_ref, bd_ref, zs_scr,
                 *, n_ctx_tiles, lat_tiles_per_seq):
    i = pl.program_id(0)
    tm = z_ref.shape[0]
    z = z_ref[...]
    row = lax.broadcasted_iota(jnp.int32, z.shape, 0)
    lane = lax.broadcasted_iota(jnp.int32, z.shape, 1)
    prev1 = jnp.where(row == 0, 0.0, pltpu.roll(z, 1, 0))
    next1 = jnp.where(row == tm - 1, 0.0, pltpu.roll(z, tm - 1, 0))

    @pl.when(i < n_ctx_tiles)
    def _():
        zs_scr[...] = jnp.where(lane % 2 == 0, prev1, next1)

    @pl.when(i >= n_ctx_tiles)
    def _():
        j = (i - n_ctx_tiles) % lat_tiles_per_seq
        col = row % GRID_W
        left = jnp.where(col == 0, 0.0, prev1)
        right = jnp.where(col == GRID_W - 1, 0.0, next1)
        up_halo = jnp.where(j == 0, 0.0, zp_ref[...])
        dn_halo = jnp.where(j == lat_tiles_per_seq - 1, 0.0, zn_ref[...])
        up = jnp.concatenate([up_halo, z[: tm - GRID_W]], axis=0)
        down = jnp.concatenate([z[GRID_W:], dn_halo], axis=0)
        m = lane % 4
        zs_scr[...] = jnp.where(m == 0, left, jnp.where(m == 1, right, jnp.where(m == 2, up, down)))

    zs = z + mu_ref[...] * (zs_scr[...] - z)
    d = D_MODEL
    r = zs[:, 0:d]
    k = zs[:, d:2 * d]
    v = zs[:, 2 * d:3 * d]
    wd = zs[:, 3 * d:3 * d + 2 * LORA]
    ad = zs[:, 3 * d + 2 * LORA:3 * d + 4 * LORA]
    gd = zs[:, 3 * d + 4 * LORA:]
    r_ref[...] = r
    v_ref[...] = v
    g_ref[...] = _dot(jax.nn.sigmoid(gd).astype(BF16), gup_ref[...])
    w_logit = w0_ref[...] + _dot(jnp.tanh(wd).astype(BF16), wup_ref[...])
    a_all = jax.nn.sigmoid(a0_ref[...] + _dot(ad.astype(BF16), aup_ref[...]))
    kk = k * kk_ref[...]
    kk = kk * lax.rsqrt(_seg_sum(kk * kk, e_ref, et_ref) + 1e-12)
    nk_ref[...] = -kk
    ka = ka_ref[...]
    kd_sum = None
    for dr in range(2):
        a = a_all[:, dr * d:(dr + 1) * d]
        lw_ref[dr] = -DECAY_SCALE * jax.nn.sigmoid(w_logit[:, dr * d:(dr + 1) * d])
        kd = k * (1.0 + (a - 1.0) * ka)
        kd_ref[dr] = kd
        bd_ref[dr] = a * kk
        kd_sum = kd if kd_sum is None else kd_sum + kd
    bv_ref[...] = _seg_sum(kd_sum * rk_ref[...] * r, e_ref, et_ref) * v


def _prep(z_rwkv, lp, n_ctx_tok, t_lat):
    n_tok = z_rwkv.shape[0]
    tm = TM_TOK
    hb = tm // GRID_W
    n_hblk = n_tok // GRID_W
    tok_spec = pl.BlockSpec((tm, D_MODEL), lambda i: (i, 0))
    dir_spec = pl.BlockSpec((2, tm, D_MODEL), lambda i: (0, i, 0))

    def const(shape):
        return pl.BlockSpec(shape, lambda i: (0,) * len(shape))

    kern = functools.partial(_prep_kernel, n_ctx_tiles=n_ctx_tok // tm, lat_tiles_per_seq=t_lat // tm)
    tok_shape = jax.ShapeDtypeStruct((n_tok, D_MODEL), F32)
    dir_shape = jax.ShapeDtypeStruct((2, n_tok, D_MODEL), F32)
    return pl.pallas_call(
        kern,
        grid=(n_tok // tm,),
        in_specs=[pl.BlockSpec((tm, C_RWKV), lambda i: (i, 0)),
                  pl.BlockSpec((GRID_W, C_RWKV), lambda i: (jnp.maximum(i * hb - 1, 0), 0)),
                  pl.BlockSpec((GRID_W, C_RWKV), lambda i: (jnp.minimum((i + 1) * hb, n_hblk - 1), 0)),
                  const((1, C_RWKV)), const((2 * LORA, 2 * D_MODEL)), const((2 * LORA, 2 * D_MODEL)),
                  const((G_LORA, D_MODEL)), const((1, 2 * D_MODEL)), const((1, 2 * D_MODEL)),
                  const((1, D_MODEL)), const((1, D_MODEL)), const((1, D_MODEL)),
                  const((D_MODEL, LANE)), const((LANE, D_MODEL))],
        out_specs=[tok_spec, tok_spec, tok_spec, tok_spec, tok_spec, dir_spec, dir_spec, dir_spec],
        out_shape=[tok_shape] * 5 + [dir_shape] * 3,
        scratch_shapes=[pltpu.VMEM((tm, C_RWKV), F32)],
        compiler_params=_cparams(("parallel",)),
        name="rwkv_prep",
    )(z_rwkv, z_rwkv, z_rwkv, lp["mu"], lp["wup"], lp["aup"], lp["gup"], lp["w0"], lp["a0"], lp["k_k"],
      lp["k_a"], lp["r_k"], lp["e"], lp["et"])


def _neumann_inverse(a, eye):
    n = a.shape[0]
    t = eye + a
    p = a
    m = 1
    while 2 * m < n:
        p = _dot3(p, p)
        t = t + _dot3(p, t)
        m *= 2
    return t


def _scan_kernel(blk_ref, first_ref, seq_ref, lw_ref, kd_ref, bd_ref, r_ref, v_ref, nk_ref, h0_ref,
                 y_ref, hT_ref, s_scr):
    del blk_ref, seq_ref
    dr = pl.program_id(0)
    s = pl.program_id(2)
    n = lw_ref.shape[1]

    @pl.when(first_ref[s] == 1)
    def _():
        s_scr[...] = h0_ref[0, 0]

    sign = 1 - 2 * dr
    row = lax.broadcasted_iota(jnp.int32, (n, n), 0)
    col = lax.broadcasted_iota(jnp.int32, (n, n), 1)
    diff = (row - col) * sign
    incl = diff >= 0
    strict = diff > 0
    eye = (row == col).astype(F32)

    lw = lw_ref[0]
    lhi, llo = _split(lw)
    tri = incl.astype(BF16)
    cum = _dot(tri, lhi) + _dot(tri, llo)
    tot = jnp.where(dr == 0, cum[n - 1:n, :], cum[0:1, :])
    g_in = jnp.exp(cum)
    g_ex = jnp.exp(cum - lw)
    g_inv = jnp.exp(-cum)
    g_rest = jnp.exp(tot - cum)
    g_tot = jnp.exp(tot)
    kd = kd_ref[0]
    bd = bd_ref[0]
    nkt = nk_ref[...] * g_ex
    rt = r_ref[...] * g_in
    bt = bd * g_inv
    kt = kd * g_inv
    bh = bd * g_rest
    kh = kd * g_rest
    v = v_ref[...]

    for h in range(LANE // HEAD):
        sl = slice(h * HEAD, (h + 1) * HEAD)
        st = s_scr[h]
        aa = _dot3(jnp.concatenate([nkt[:, sl], rt[:, sl]], axis=0),
                   jnp.concatenate([bt[:, sl], kt[:, sl]], axis=0), _NT)
        a_b = jnp.where(strict, aa[:n, :n], 0.0)
        a_k = jnp.where(strict, aa[:n, n:], 0.0)
        q_b = jnp.where(incl, aa[n:, :n], 0.0)
        q_k = jnp.where(incl, aa[n:, n:], 0.0)
        vh = v[:, sl]
        t_inv = _neumann_inverse(a_b, eye)
        wu = _dot3(t_inv, jnp.concatenate([nkt[:, sl], _dot3(a_k, vh)], axis=1))
        gs = _dot3(jnp.concatenate([wu[:, :HEAD], rt[:, sl]], axis=0), st, _NT)
        u = gs[:n] + wu[:, HEAD:]
        uv = jnp.concatenate([u, vh], axis=0)
        y_ref[0, :, sl] = gs[n:] + _dot3(jnp.concatenate([q_b, q_k], axis=1), uv)
        s_scr[h] = st * g_tot[:, sl] + _dot3(uv, jnp.concatenate([bh[:, sl], kh[:, sl]], axis=0), _TN)

    hT_ref[0, 0] = s_scr[...]


def _scan_tables(seqs):
    blk = [[], []]
    first, seq = [], []
    sid = 0
    for tok0, nb, t in seqs:
        nc = t // SCAN_L
        for b in range(nb):
            base = (tok0 + b * t) // SCAN_L
            for c in range(nc):
                blk[0].append(base + c)
                blk[1].append(base + nc - 1 - c)
                first.append(1 if c == 0 else 0)
                seq.append(sid)
            sid += 1
    return (jnp.asarray(np.array(blk, np.int32).reshape(-1)), jnp.asarray(np.array(first, np.int32)),
            jnp.asarray(np.array(seq, np.int32)), len(first))


def _scan(lw, kd, bd, r, v, nk, h0, seqs):
    n_tok = r.shape[0]
    n_seq = h0.shape[0]
    blk, first, seq, n_steps = _scan_tables(seqs)
    hp = LANE // HEAD
    n_pair = N_HEAD // hp

    def dir_map(d, p, s, blk, first, seq):
        return (d, blk[d * n_steps + s], p)

    def tok_map(d, p, s, blk, first, seq):
        return (blk[d * n_steps + s], p)

    def st_map(d, p, s, blk, first, seq):
        return (seq[s], d, p, 0, 0)

    dir_spec = pl.BlockSpec((1, SCAN_L, LANE), dir_map)
    tok_spec = pl.BlockSpec((SCAN_L, LANE), tok_map)
    st_spec = pl.BlockSpec((1, 1, hp, HEAD, HEAD), st_map)
    return pl.pallas_call(
        _scan_kernel,
        grid_spec=pltpu.PrefetchScalarGridSpec(
            num_scalar_prefetch=3,
            grid=(2, n_pair, n_steps),
            in_specs=[dir_spec, dir_spec, dir_spec, tok_spec, tok_spec, tok_spec, st_spec],
            out_specs=[dir_spec, st_spec],
            scratch_shapes=[pltpu.VMEM((hp, HEAD, HEAD), F32)]),
        out_shape=[jax.ShapeDtypeStruct((2, n_tok, D_MODEL), F32),
                   jax.ShapeDtypeStruct((n_seq, 2, N_HEAD, HEAD, HEAD), F32)],
        compiler_params=_cparams(("parallel", "parallel", "arbitrary")),
        name="rwkv_scan",
    )(blk, first, seq, lw, kd, bd, r, v, nk, h0)


def _post_kernel(y0_ref, y1_ref, bv_ref, g_ref, lg_ref, lb_ref, e_ref, et_ref, w_ref, o_ref):
    y = y0_ref[0] + y1_ref[0]
    mu = _seg_sum(y, e_ref, et_ref) * (1.0 / HEAD)
    yc = y - mu
    var = _seg_sum(yc * yc, e_ref, et_ref) * (1.0 / HEAD)
    yn = yc * lax.rsqrt(var + GN_EPS) * lg_ref[...] + lb_ref[...]
    o_ref[...] = _dot(((yn + bv_ref[...]) * g_ref[...]).astype(BF16), w_ref[...])


def _post(y, bv, g, lp):
    n_tok = bv.shape[0]
    tm = TM_TOK
    tok_spec = pl.BlockSpec((tm, D_MODEL), lambda i: (i, 0))

    def const(shape):
        return pl.BlockSpec(shape, lambda i: (0,) * len(shape))

    return pl.pallas_call(
        _post_kernel,
        grid=(n_tok // tm,),
        in_specs=[pl.BlockSpec((1, tm, D_MODEL), lambda i: (0, i, 0)),
                  pl.BlockSpec((1, tm, D_MODEL), lambda i: (1, i, 0)),
                  tok_spec, tok_spec, const((1, D_MODEL)), const((1, D_MODEL)),
                  const((D_MODEL, LANE)), const((LANE, D_MODEL)), const((D_MODEL, D_MODEL))],
        out_specs=tok_spec,
        out_shape=jax.ShapeDtypeStruct((n_tok, D_MODEL), F32),
        compiler_params=_cparams(("parallel",)),
        name="rwkv_post",
    )(y, y, bv, g, lp["lnx_g"], lp["lnx_b"], lp["e"], lp["et"], lp["w_branch_a"])


def _cmlp_kernel(zu_ref, zv_ref, lg_ref, ws_ref, bs_ref, w_ref, o_ref, y_scr):
    v = jax.nn.gelu(zv_ref[...])
    mu = jnp.mean(v, axis=-1, keepdims=True)
    vc = v - mu
    var = jnp.mean(vc * vc, axis=-1, keepdims=True)
    vn = (vc * lax.rsqrt(var + EPS) * lg_ref[...]).astype(BF16)
    u = jax.nn.gelu(zu_ref[...])
    for c in range(zu_ref.shape[0] // CHUNK):
        rows = slice(c * CHUNK, (c + 1) * CHUNK)
        for h in range(H_B):
            cols = slice(h * HEAD_B, (h + 1) * HEAD_B)
            s = _dot(ws_ref[h], vn[rows, cols]) + bs_ref[:, cols]
            y_scr[rows, cols] = (u[rows, cols] * s).astype(BF16)
    o_ref[...] = _dot(y_scr[...], w_ref[...])


def _cmlp(z_rest, lp):
    n_tok = z_rest.shape[0]
    tm = TM_TOK

    def const(shape):
        return pl.BlockSpec(shape, lambda i: (0,) * len(shape))

    return pl.pallas_call(
        _cmlp_kernel,
        grid=(n_tok // tm,),
        in_specs=[pl.BlockSpec((tm, D_MODEL), lambda i: (i, 0)),
                  pl.BlockSpec((tm, D_MODEL), lambda i: (i, 1)),
                  const((1, D_MODEL)), const((H_B, CHUNK, CHUNK)), const((CHUNK, D_MODEL)),
                  const((D_MODEL, D_MODEL))],
        out_specs=pl.BlockSpec((tm, D_MODEL), lambda i: (i, 0)),
        out_shape=jax.ShapeDtypeStruct((n_tok, D_MODEL), F32),
        scratch_shapes=[pltpu.VMEM((tm, D_MODEL), BF16)],
        compiler_params=_cparams(("parallel",)),
        name="chunk_mlp",
    )(z_rest, z_rest, lp["ln_v_g"], lp["w_s"], lp["b_s"], lp["w_branch_b"])


def _ffn_kernel(x_ref, ya_ref, yb_ref, ga_ref, gb_ref, mod_ref, g2_ref, wo_ref, w1_ref, w2_ref, fg_ref,
                o_ref, *maybe_final, final):
    m = mod_ref[0]
    mixed = jax.nn.sigmoid(ga_ref[...]) * ya_ref[...] + jax.nn.sigmoid(gb_ref[...]) * yb_ref[...]
    x = x_ref[...] + m[2:3] * _dot(mixed.astype(BF16), wo_ref[...])
    h2 = _rms_mod(x, g2_ref[...], m[3:4], m[4:5]).astype(BF16)
    acc = jnp.zeros(x.shape, F32)
    ff_chunk = D_MODEL
    for c in range(D_FF // ff_chunk):
        cols = slice(c * ff_chunk, (c + 1) * ff_chunk)
        hid = jnp.square(jnp.maximum(_dot(h2, w1_ref[:, cols]), 0.0)).astype(BF16)
        acc = acc + _dot(hid, w2_ref[cols, :])
    x = x + m[5:6] * acc
    o_ref[...] = x
    if final:
        yf = x * lax.rsqrt(jnp.mean(x * x, axis=-1, keepdims=True) + EPS) * fg_ref[...]
        maybe_final[0][...] = yf


def _ffn(x, ya, yb, z_rest, mod_l, lp, final_g, mod_row, final):
    n_tok = x.shape[0]
    tm = TM_TOK
    tok_spec = pl.BlockSpec((tm, D_MODEL), lambda i: (i, 0))

    def const(shape):
        return pl.BlockSpec(shape, lambda i: (0,) * len(shape), pipeline_mode=pl.Buffered(1))

    tok_shape = jax.ShapeDtypeStruct((n_tok, D_MODEL), F32)
    return pl.pallas_call(
        functools.partial(_ffn_kernel, final=final),
        grid=(n_tok // tm,),
        in_specs=[tok_spec, tok_spec, tok_spec,
                  pl.BlockSpec((tm, D_MODEL), lambda i: (i, 2)),
                  pl.BlockSpec((tm, D_MODEL), lambda i: (i, 3)),
                  pl.BlockSpec((1, N_MOD, D_MODEL), lambda i: (mod_row(i), 0, 0)),
                  const((1, D_MODEL)), const((D_MODEL, D_MODEL)), const((D_MODEL, D_FF)),
                  const((D_FF, D_MODEL)), const((1, D_MODEL))],
        out_specs=[tok_spec, tok_spec] if final else [tok_spec],
        out_shape=[tok_shape, tok_shape] if final else [tok_shape],
        compiler_params=_cparams(("parallel",)),
        name="mix_ffn",
    )(x, ya, yb, z_rest, z_rest, mod_l, lp["norm2_g"], lp["w_out"], lp["w1"], lp["w2"], final_g)


def _block_diag2(m):
    z = jnp.zeros_like(m[0])
    return jnp.concatenate([jnp.concatenate([m[0], z], axis=1), jnp.concatenate([z, m[1]], axis=1)], axis=0)


def _layer_params(l, w_in, mu_shift, w0, w_up, a0, a_up, g_up, k_k, k_a, r_k, lnx_g, lnx_b, w_branch_a,
                  ln_v_g, w_s, b_s, w_branch_b, w_out, w1, w2, norm1_g, norm2_g):
    head_of = np.arange(D_MODEL) // HEAD
    e = (head_of[:, None] == np.arange(LANE)[None, :]).astype(np.float32)
    row = lambda a: a.reshape(1, -1)
    return dict(
        w_in_rwkv=w_in[l][:, :C_RWKV].astype(BF16), w_in_rest=w_in[l][:, C_RWKV:].astype(BF16),
        mu=row(mu_shift[l]), wup=_block_diag2(w_up[l]).astype(BF16), aup=_block_diag2(a_up[l]).astype(BF16),
        gup=g_up[l].astype(BF16), w0=row(w0[l]), a0=row(a0[l]), k_k=row(k_k[l]), k_a=row(k_a[l]),
        r_k=row(r_k[l]), lnx_g=row(lnx_g[l]), lnx_b=row(lnx_b[l]), w_branch_a=w_branch_a[l].astype(BF16),
        ln_v_g=row(ln_v_g[l]), w_s=w_s[l].astype(BF16), b_s=jnp.repeat(b_s[l].T, HEAD_B, axis=1),
        w_branch_b=w_branch_b[l].astype(BF16), w_out=w_out[l].astype(BF16), w1=w1[l].astype(BF16),
        w2=w2[l].astype(BF16), norm1_g=row(norm1_g[l]), norm2_g=row(norm2_g[l]),
        e=jnp.asarray(e, BF16), et=jnp.asarray(e.T, BF16))


def kernel(x_prompt, x_sample, state_rwkv, c, c_ctx, w_ada, b_ada, norm1_g, norm2_g, w_in, mu_shift, w0, w_up,
           a0, a_up, g_up, k_k, k_a, r_k, lnx_g, lnx_b, w_branch_a, ln_v_g, w_s, b_s, w_branch_b, w_out, w1, w2,
           final_g):
    b_ctx, t_ctx, _ = x_prompt.shape
    b_lat, t_lat, _ = x_sample.shape
    depth = w_in.shape[0]
    n_ctx = b_ctx * t_ctx
    n_lat = b_lat * t_lat
    assert t_ctx == TM_TOK and t_lat % TM_TOK == 0 and t_lat % GRID_W == 0

    x = jnp.concatenate([x_prompt.reshape(n_ctx, D_MODEL), x_sample.reshape(n_lat, D_MODEL)], axis=0)
    cond = jnp.concatenate([c_ctx[None, :], c], axis=0)
    mod = _modulation(cond, w_ada, b_ada).reshape(depth, 1 + b_lat, N_MOD, D_MODEL)
    seqs = [(0, b_ctx, t_ctx), (n_ctx, b_lat, t_lat)]
    s_zero = jnp.zeros((b_ctx, 2, N_HEAD, HEAD, HEAD), F32)
    final_row = final_g.reshape(1, D_MODEL)

    states = []
    y_final = None
    for l in range(depth):
        lp = _layer_params(l, w_in, mu_shift, w0, w_up, a0, a_up, g_up, k_k, k_a, r_k, lnx_g, lnx_b,
                           w_branch_a, ln_v_g, w_s, b_s, w_branch_b, w_out, w1, w2, norm1_g, norm2_g)
        row512 = _mod_row_map(n_ctx, t_lat, 512)
        row_tm = _mod_row_map(n_ctx, t_lat, TM_TOK)
        z_rwkv = _in_proj(x, mod[l], lp["norm1_g"], lp["w_in_rwkv"], 1152, row512)
        z_rest = _in_proj(x, mod[l], lp["norm1_g"], lp["w_in_rest"], 1024, row512)
        r, v, nk, g, bv, lw, kd, bd = _prep(z_rwkv, lp, n_ctx, t_lat)
        h0 = jnp.concatenate([s_zero, state_rwkv[:, l]], axis=0)
        y, h_fin = _scan(lw, kd, bd, r, v, nk, h0, seqs)
        states.append(h_fin[:b_ctx])
        ya = _post(y, bv, g, lp)
        yb = _cmlp(z_rest, lp)
        outs = _ffn(x, ya, yb, z_rest, mod[l], lp, final_row, row_tm, final=(l == depth - 1))
        x = outs[0]
        if l == depth - 1:
            y_final = outs[1]

    y_prompt = y_final[:n_ctx].reshape(b_ctx, t_ctx, D_MODEL)
    y_sample = y_final[n_ctx:].reshape(b_lat, t_lat, D_MODEL)
    return (y_prompt, y_sample, jnp.stack(states, axis=1))
```

```python
import functools
import math

import numpy as np
import jax
import jax.numpy as jnp
from jax import lax
from jax.experimental import pallas as pl
from jax.experimental.pallas import tpu as pltpu

F32 = jnp.float32
BF16 = jnp.bfloat16

D_MODEL = 1024
HEAD = 64
N_HEAD = D_MODEL // HEAD
LORA = 64
G_LORA = 128
C_RWKV = 3 * D_MODEL + 4 * LORA + G_LORA
D_REST = 4 * D_MODEL
D_FF = 4 * D_MODEL
GRID_W = 64
CHUNK = 128
H_B = 8
HEAD_B = D_MODEL // H_B
N_MOD = 6
EPS = 1e-6
GN_EPS = 64e-5
DECAY_SCALE = math.exp(-0.5)

SCAN_L = 64
LANE = 128
SCAN_W = 1024
TM_TOK = 256
VMEM_LIMIT = 56 * 1024 * 1024


def _cparams(sem):
    return pltpu.CompilerParams(dimension_semantics=sem, vmem_limit_bytes=VMEM_LIMIT)


def _split(x):
    hi = x.astype(BF16)
    lo = (x - hi.astype(F32)).astype(BF16)
    return hi, lo


def _dot(a, b, dims=(((1,), (0,)), ((), ()))):
    return lax.dot_general(a, b, dims, preferred_element_type=F32)


_NN = (((1,), (0,)), ((), ()))
_NT = (((1,), (1,)), ((), ()))
_TN = (((0,), (0,)), ((), ()))


def _dot3(a, b, dims=_NN):
    ah, al = _split(a)
    bh, bl = _split(b)
    return _dot(ah, bh, dims) + (_dot(ah, bl, dims) + _dot(al, bh, dims))


def _seg_sum(x, e_ref, et_ref):
    hi, lo = _split(x)
    s = _dot(hi, e_ref[...]) + _dot(lo, e_ref[...])
    shi, slo = _split(s)
    return _dot(shi, et_ref[...]) + _dot(slo, et_ref[...])


def _mod_kernel(c_ref, w_ref, b_ref, o_ref):
    cond = c_ref[...]
    o_ref[0] = _dot(jax.nn.silu(cond).astype(BF16), w_ref[0].astype(BF16)) + b_ref[0]


def _modulation(cond, w_ada, b_ada):
    depth = w_ada.shape[0]
    n = cond.shape[0]
    tn = 1536
    return pl.pallas_call(
        _mod_kernel,
        grid=(depth, (N_MOD * D_MODEL) // tn),
        in_specs=[pl.BlockSpec((n, D_MODEL), lambda l, j: (0, 0)),
                  pl.BlockSpec((1, D_MODEL, tn), lambda l, j: (l, 0, j)),
                  pl.BlockSpec((1, 1, tn), lambda l, j: (l, 0, j))],
        out_specs=pl.BlockSpec((1, n, tn), lambda l, j: (l, 0, j)),
        out_shape=jax.ShapeDtypeStruct((depth, n, N_MOD * D_MODEL), F32),
        compiler_params=_cparams(("parallel", "parallel")),
        name="modulation",
    )(cond, w_ada, b_ada.reshape(depth, 1, N_MOD * D_MODEL))


def _mod_row_map(n_ctx_tok, t_lat, tm):
    def row(i):
        tok = i * tm
        return jnp.where(tok < n_ctx_tok, 0, 1 + (tok - n_ctx_tok) // t_lat)
    return row


def _rms_mod(x, g, shift, scale):
    y = x * lax.rsqrt(jnp.mean(x * x, axis=-1, keepdims=True) + EPS) * g
    return y * (1.0 + scale) + shift


def _in_proj_kernel(x_ref, mod_ref, g_ref, w_ref, o_ref, h_scr):
    @pl.when(pl.program_id(1) == 0)
    def _():
        m = mod_ref[0]
        h_scr[...] = _rms_mod(x_ref[...], g_ref[...], m[0:1], m[1:2]).astype(BF16)

    o_ref[...] = _dot(h_scr[...], w_ref[...])


def _in_proj(x, mod_l, g, w, tn, mod_row, tm=512):
    n_tok = x.shape[0]
    n_out = w.shape[1]
    return pl.pallas_call(
        _in_proj_kernel,
        grid=(n_tok // tm, n_out // tn),
        in_specs=[pl.BlockSpec((tm, D_MODEL), lambda i, j: (i, 0)),
                  pl.BlockSpec((1, N_MOD, D_MODEL), lambda i, j: (mod_row(i), 0, 0)),
                  pl.BlockSpec((1, D_MODEL), lambda i, j: (0, 0)),
                  pl.BlockSpec((D_MODEL, tn), lambda i, j: (0, j))],
        out_specs=pl.BlockSpec((tm, tn), lambda i, j: (i, j)),
        out_shape=jax.ShapeDtypeStruct((n_tok, n_out), F32),
        scratch_shapes=[pltpu.VMEM((tm, D_MODEL), BF16)],
        compiler_params=_cparams(("parallel", "arbitrary")),
        name="in_proj",
    )(x, mod_l, g, w)


def _shift_rows(z, k, fill_first):
    return jnp.concatenate([fill_first, z[: z.shape[0] - k]], axis=0)


def _prep_kernel(z_ref, zp_ref, zn_ref, mu_ref, wup_ref, aup_ref, gup_ref, w0_ref, a0_ref, kk_ref, ka_ref,
                 rk_ref, e_ref, et_ref,
                 r_ref, v_ref, nk_ref, g_ref, bv_ref, lw_ref, kd_ref, bd_ref, zs_scr,
                 *, n_ctx_tiles, lat_tiles_per_seq):
    i = pl.program_id(0)
    tm = z_ref.shape[0]
    z = z_ref[...]
    row = lax.broadcasted_iota(jnp.int32, z.shape, 0)
    lane = lax.broadcasted_iota(jnp.int32, z.shape, 1)
    prev1 = jnp.where(row == 0, 0.0, pltpu.roll(z, 1, 0))
    next1 = jnp.where(row == tm - 1, 0.0, pltpu.roll(z, tm - 1, 0))

    @pl.when(i < n_ctx_tiles)
    def _():
        zs_scr[...] = jnp.where(lane % 2 == 0, prev1, next1)

    @pl.when(i >= n_ctx_tiles)
    def _():
        j = (i - n_ctx_tiles) % lat_tiles_per_seq
        col = row % GRID_W
        left = jnp.where(col == 0, 0.0, prev1)
        right = jnp.where(col == GRID_W - 1, 0.0, next1)
        up_halo = jnp.where(j == 0, 0.0, zp_ref[...])
        dn_halo = jnp.where(j == lat_tiles_per_seq - 1, 0.0, zn_ref[...])
        up = jnp.concatenate([up_halo, z[: tm - GRID_W]], axis=0)
        down = jnp.concatenate([z[GRID_W:], dn_halo], axis=0)
        m = lane % 4
        zs_scr[...] = jnp.where(m == 0, left, jnp.where(m == 1, right, jnp.where(m == 2, up, down)))

    zs = z + mu_ref[...] * (zs_scr[...] - z)
    d = D_MODEL
    r = zs[:, 0:d]
    k = zs[:, d:2 * d]
    v = zs[:, 2 * d:3 * d]
    wd = zs[:, 3 * d:3 * d + 2 * LORA]
    ad = zs[:, 3 * d + 2 * LORA:3 * d + 4 * LORA]
    gd = zs[:, 3 * d + 4 * LORA:]
    r_ref[...] = r
    v_ref[...] = v
    g_ref[...] = _dot(jax.nn.sigmoid(gd).astype(BF16), gup_ref[...])
    w_logit = w0_ref[...] + _dot(jnp.tanh(wd).astype(BF16), wup_ref[...])
    a_all = jax.nn.sigmoid(a0_ref[...] + _dot(ad.astype(BF16), aup_ref[...]))
    kk = k * kk_ref[...]
    kk = kk * lax.rsqrt(_seg_sum(kk * kk, e_ref, et_ref) + 1e-12)
    nk_ref[...] = -kk
    ka = ka_ref[...]
    kd_sum = None
    for dr in range(2):
        a = a_all[:, dr * d:(dr + 1) * d]
        lw_ref[dr] = -DECAY_SCALE * jax.nn.sigmoid(w_logit[:, dr * d:(dr + 1) * d])
        kd = k * (1.0 + (a - 1.0) * ka)
        kd_ref[dr] = kd
        bd_ref[dr] = a * kk
        kd_sum = kd if kd_sum is None else kd_sum + kd
    bv_ref[...] = _seg_sum(kd_sum * rk_ref[...] * r, e_ref, et_ref) * v


def _prep(z_rwkv, lp, n_ctx_tok, t_lat):
    n_tok = z_rwkv.shape[0]
    tm = TM_TOK
    hb = tm // GRID_W
    n_hblk = n_tok // GRID_W
    tok_spec = pl.BlockSpec((tm, D_MODEL), lambda i: (i, 0))
    dir_spec = pl.BlockSpec((2, tm, D_MODEL), lambda i: (0, i, 0))

    def const(shape):
        return pl.BlockSpec(shape, lambda i: (0,) * len(shape))

    kern = functools.partial(_prep_kernel, n_ctx_tiles=n_ctx_tok // tm, lat_tiles_per_seq=t_lat // tm)
    tok_shape = jax.ShapeDtypeStruct((n_tok, D_MODEL), F32)
    dir_shape = jax.ShapeDtypeStruct((2, n_tok, D_MODEL), F32)
    return pl.pallas_call(
        kern,
        grid=(n_tok // tm,),
        in_specs=[pl.BlockSpec((tm, C_RWKV), lambda i: (i, 0)),
                  pl.BlockSpec((GRID_W, C_RWKV), lambda i: (jnp.maximum(i * hb - 1, 0), 0)),
                  pl.BlockSpec((GRID_W, C_RWKV), lambda i: (jnp.minimum((i + 1) * hb, n_hblk - 1), 0)),
                  const((1, C_RWKV)), const((2 * LORA, 2 * D_MODEL)), const((2 * LORA, 2 * D_MODEL)),
                  const((G_LORA, D_MODEL)), const((1, 2 * D_MODEL)), const((1, 2 * D_MODEL)),
                  const((1, D_MODEL)), const((1, D_MODEL)), const((1, D_MODEL)),
                  const((D_MODEL, LANE)), const((LANE, D_MODEL))],
        out_specs=[tok_spec, tok_spec, tok_spec, tok_spec, tok_spec, dir_spec, dir_spec, dir_spec],
        out_shape=[tok_shape] * 5 + [dir_shape] * 3,
        scratch_shapes=[pltpu.VMEM((tm, C_RWKV), F32)],
        compiler_params=_cparams(("parallel",)),
        name="rwkv_prep",
    )(z_rwkv, z_rwkv, z_rwkv, lp["mu"], lp["wup"], lp["aup"], lp["gup"], lp["w0"], lp["a0"], lp["k_k"],
      lp["k_a"], lp["r_k"], lp["e"], lp["et"])


def _mm(a, b, dims=_NN, passes=3):
    out = _dot(a[0], b[0], dims)
    if passes == 3:
        out = out + (_dot(a[0], b[1], dims) + _dot(a[1], b[0], dims))
    return out


def _cols(p, sl):
    return (p[0][:, sl], p[1][:, sl])


def _rows(p, sl):
    return (p[0][sl], p[1][sl])


def _cat(ps, axis):
    return (jnp.concatenate([p[0] for p in ps], axis=axis), jnp.concatenate([p[1] for p in ps], axis=axis))


def _scan_kernel(blk_ref, first_ref, seq_ref, lw_ref, kd_ref, bd_ref, r_ref, v_ref, nk_ref, h0_ref,
                 y_ref, hT_ref, s_scr):
    del blk_ref, seq_ref
    dr = pl.program_id(0)
    s = pl.program_id(2)
    n = lw_ref.shape[1]
    heads = range(lw_ref.shape[2] // HEAD)
    hs = [slice(h * HEAD, (h + 1) * HEAD) for h in heads]
    lo_half = slice(0, n)
    hi_half = slice(n, 2 * n)

    @pl.when(first_ref[s] == 1)
    def _():
        s_scr[...] = h0_ref[0, 0]

    sign = 1 - 2 * dr
    row = lax.broadcasted_iota(jnp.int32, (n, n), 0)
    col = lax.broadcasted_iota(jnp.int32, (n, n), 1)
    tri = ((row - col) * sign >= 0).astype(BF16)
    row2 = lax.broadcasted_iota(jnp.int32, (n, 2 * n), 0)
    lane2 = lax.broadcasted_iota(jnp.int32, (n, 2 * n), 1)
    left = lane2 < n
    diff2 = (row2 - jnp.where(left, lane2, lane2 - n)) * sign
    incl2 = diff2 >= 0
    strict2 = diff2 > 0
    eye_right = jnp.where(jnp.logical_and(jnp.logical_not(left), diff2 == 0), 1.0, 0.0)

    lw = lw_ref[0]
    lhi, llo = _split(lw)
    cum = _dot(tri, lhi) + _dot(tri, llo)
    tot = jnp.where(dr == 0, cum[n - 1:n, :], cum[0:1, :])
    g_inv = jnp.exp(-cum)
    g_rest = jnp.exp(tot - cum)
    g_tot = jnp.exp(tot)
    kd = kd_ref[0]
    bd = bd_ref[0]
    v = v_ref[...]
    nr = _split(jnp.concatenate([nk_ref[...] * jnp.exp(cum - lw), r_ref[...] * jnp.exp(cum)], axis=0))
    bk = _split(jnp.concatenate([bd * g_inv, kd * g_inv], axis=0))
    bkh = _split(jnp.concatenate([bd * g_rest, kd * g_rest], axis=0))
    vs = _split(v)

    aa = [_mm(_cols(nr, c), _cols(bk, c), _NT) for c in hs]
    top = [jnp.where(strict2, a[:n], 0.0) for a in aa]
    bot = [_split(jnp.where(incl2, a[n:], 0.0)) for a in aa]
    tops = [_split(t) for t in top]
    x = [_mm(_cols(tp, hi_half), _cols(vs, c)) for tp, c in zip(tops, hs)]
    slab = [jnp.where(left, t, eye_right) for t in top]
    m = 1
    while m < n:
        sp = [_split(sb) for sb in slab]
        slab = [_mm(_cols(p, lo_half), p) + jnp.where(left, 0.0, sb) for p, sb in zip(sp, slab)]
        m *= 2
    sp = [_split(sb) for sb in slab]
    rhs = [_cat([_rows(_cols(nr, c), lo_half), _split(xh)], axis=1) for c, xh in zip(hs, x)]
    wu = [_mm(_cols(p, hi_half), q) for p, q in zip(sp, rhs)]
    st = [s_scr[h] for h in heads]
    gs = [_mm(_cat([_cols(_split(w), lo_half), _rows(_cols(nr, c), hi_half)], axis=0), _split(sh), _NT)
          for w, c, sh in zip(wu, hs, st)]
    uv = [_split(jnp.concatenate([g[:n] + w[:, hi_half], v[:, c]], axis=0)) for g, w, c in zip(gs, wu, hs)]
    for h in heads:
        y_ref[0, :, hs[h]] = gs[h][n:] + _mm(bot[h], uv[h])
    for h in heads:
        s_scr[h] = st[h] * g_tot[:, hs[h]] + _mm(uv[h], _cols(bkh, hs[h]), _TN)

    hT_ref[0, 0] = s_scr[...]


def _scan_tables(seqs):
    blk = [[], []]
    first, seq = [], []
    sid = 0
    for tok0, nb, t in seqs:
        nc = t // SCAN_L
        for b in range(nb):
            base = (tok0 + b * t) // SCAN_L
            for c in range(nc):
                blk[0].append(base + c)
                blk[1].append(base + nc - 1 - c)
                first.append(1 if c == 0 else 0)
                seq.append(sid)
            sid += 1
    return (jnp.asarray(np.array(blk, np.int32).reshape(-1)), jnp.asarray(np.array(first, np.int32)),
            jnp.asarray(np.array(seq, np.int32)), len(first))


def _scan(lw, kd, bd, r, v, nk, h0, seqs):
    n_tok = r.shape[0]
    n_seq = h0.shape[0]
    blk, first, seq, n_steps = _scan_tables(seqs)
    hp = SCAN_W // HEAD
    n_pair = N_HEAD // hp

    def dir_map(d, p, s, blk, first, seq):
        return (d, blk[d * n_steps + s], p)

    def tok_map(d, p, s, blk, first, seq):
        return (blk[d * n_steps + s], p)

    def st_map(d, p, s, blk, first, seq):
        return (seq[s], d, p, 0, 0)

    dir_spec = pl.BlockSpec((1, SCAN_L, SCAN_W), dir_map)
    tok_spec = pl.BlockSpec((SCAN_L, SCAN_W), tok_map)
    st_spec = pl.BlockSpec((1, 1, hp, HEAD, HEAD), st_map)
    return pl.pallas_call(
        _scan_kernel,
        grid_spec=pltpu.PrefetchScalarGridSpec(
            num_scalar_prefetch=3,
            grid=(2, n_pair, n_steps),
            in_specs=[dir_spec, dir_spec, dir_spec, tok_spec, tok_spec, tok_spec, st_spec],
            out_specs=[dir_spec, st_spec],
            scratch_shapes=[pltpu.VMEM((hp, HEAD, HEAD), F32)]),
        out_shape=[jax.ShapeDtypeStruct((2, n_tok, D_MODEL), F32),
                   jax.ShapeDtypeStruct((n_seq, 2, N_HEAD, HEAD, HEAD), F32)],
        compiler_params=_cparams(("parallel", "parallel", "arbitrary")),
        name="rwkv_scan",
    )(blk, first, seq, lw, kd, bd, r, v, nk, h0)


def _post_kernel(y0_ref, y1_ref, bv_ref, g_ref, lg_ref, lb_ref, e_ref, et_ref, w_ref, o_ref):
    y = y0_ref[0] + y1_ref[0]
    mu = _seg_sum(y, e_ref, et_ref) * (1.0 / HEAD)
    yc = y - mu
    var = _seg_sum(yc * yc, e_ref, et_ref) * (1.0 / HEAD)
    yn = yc * lax.rsqrt(var + GN_EPS) * lg_ref[...] + lb_ref[...]
    o_ref[...] = _dot(((yn + bv_ref[...]) * g_ref[...]).astype(BF16), w_ref[...])


def _post(y, bv, g, lp):
    n_tok = bv.shape[0]
    tm = TM_TOK
    tok_spec = pl.BlockSpec((tm, D_MODEL), lambda i: (i, 0))

    def const(shape):
        return pl.BlockSpec(shape, lambda i: (0,) * len(shape))

    return pl.pallas_call(
        _post_kernel,
        grid=(n_tok // tm,),
        in_specs=[pl.BlockSpec((1, tm, D_MODEL), lambda i: (0, i, 0)),
                  pl.BlockSpec((1, tm, D_MODEL), lambda i: (1, i, 0)),
                  tok_spec, tok_spec, const((1, D_MODEL)), const((1, D_MODEL)),
                  const((D_MODEL, LANE)), const((LANE, D_MODEL)), const((D_MODEL, D_MODEL))],
        out_specs=tok_spec,
        out_shape=jax.ShapeDtypeStruct((n_tok, D_MODEL), F32),
        compiler_params=_cparams(("parallel",)),
        name="rwkv_post",
    )(y, y, bv, g, lp["lnx_g"], lp["lnx_b"], lp["e"], lp["et"], lp["w_branch_a"])


def _cmlp_kernel(zu_ref, zv_ref, lg_ref, ws_ref, bs_ref, w_ref, o_ref, y_scr):
    v = jax.nn.gelu(zv_ref[...])
    mu = jnp.mean(v, axis=-1, keepdims=True)
    vc = v - mu
    var = jnp.mean(vc * vc, axis=-1, keepdims=True)
    vn = (vc * lax.rsqrt(var + EPS) * lg_ref[...]).astype(BF16)
    u = jax.nn.gelu(zu_ref[...])
    for c in range(zu_ref.shape[0] // CHUNK):
        rows = slice(c * CHUNK, (c + 1) * CHUNK)
        for h in range(H_B):
            cols = slice(h * HEAD_B, (h + 1) * HEAD_B)
            s = _dot(ws_ref[h], vn[rows, cols]) + bs_ref[:, cols]
            y_scr[rows, cols] = (u[rows, cols] * s).astype(BF16)
    o_ref[...] = _dot(y_scr[...], w_ref[...])


def _cmlp(z_rest, lp):
    n_tok = z_rest.shape[0]
    tm = TM_TOK

    def const(shape):
        return pl.BlockSpec(shape, lambda i: (0,) * len(shape))

    return pl.pallas_call(
        _cmlp_kernel,
        grid=(n_tok // tm,),
        in_specs=[pl.BlockSpec((tm, D_MODEL), lambda i: (i, 0)),
                  pl.BlockSpec((tm, D_MODEL), lambda i: (i, 1)),
                  const((1, D_MODEL)), const((H_B, CHUNK, CHUNK)), const((CHUNK, D_MODEL)),
                  const((D_MODEL, D_MODEL))],
        out_specs=pl.BlockSpec((tm, D_MODEL), lambda i: (i, 0)),
        out_shape=jax.ShapeDtypeStruct((n_tok, D_MODEL), F32),
        scratch_shapes=[pltpu.VMEM((tm, D_MODEL), BF16)],
        compiler_params=_cparams(("parallel",)),
        name="chunk_mlp",
    )(z_rest, z_rest, lp["ln_v_g"], lp["w_s"], lp["b_s"], lp["w_branch_b"])


def _ffn_kernel(x_ref, ya_ref, yb_ref, ga_ref, gb_ref, mod_ref, g2_ref, wo_ref, w1_ref, w2_ref, fg_ref,
                o_ref, *maybe_final, final):
    m = mod_ref[0]
    mixed = jax.nn.sigmoid(ga_ref[...]) * ya_ref[...] + jax.nn.sigmoid(gb_ref[...]) * yb_ref[...]
    x = x_ref[...] + m[2:3] * _dot(mixed.astype(BF16), wo_ref[...])
    h2 = _rms_mod(x, g2_ref[...], m[3:4], m[4:5]).astype(BF16)
    acc = jnp.zeros(x.shape, F32)
    ff_chunk = D_MODEL
    for c in range(D_FF // ff_chunk):
        cols = slice(c * ff_chunk, (c + 1) * ff_chunk)
        hid = jnp.square(jnp.maximum(_dot(h2, w1_ref[:, cols]), 0.0)).astype(BF16)
        acc = acc + _dot(hid, w2_ref[cols, :])
    x = x + m[5:6] * acc
    o_ref[...] = x
    if final:
        yf = x * lax.rsqrt(jnp.mean(x * x, axis=-1, keepdims=True) + EPS) * fg_ref[...]
        maybe_final[0][...] = yf


def _ffn(x, ya, yb, z_rest, mod_l, lp, final_g, mod_row, final):
    n_tok = x.shape[0]
    tm = TM_TOK
    tok_spec = pl.BlockSpec((tm, D_MODEL), lambda i: (i, 0))

    def const(shape):
        return pl.BlockSpec(shape, lambda i: (0,) * len(shape), pipeline_mode=pl.Buffered(1))

    tok_shape = jax.ShapeDtypeStruct((n_tok, D_MODEL), F32)
    return pl.pallas_call(
        functools.partial(_ffn_kernel, final=final),
        grid=(n_tok // tm,),
        in_specs=[tok_spec, tok_spec, tok_spec,
                  pl.BlockSpec((tm, D_MODEL), lambda i: (i, 2)),
                  pl.BlockSpec((tm, D_MODEL), lambda i: (i, 3)),
                  pl.BlockSpec((1, N_MOD, D_MODEL), lambda i: (mod_row(i), 0, 0)),
                  const((1, D_MODEL)), const((D_MODEL, D_MODEL)), const((D_MODEL, D_FF)),
                  const((D_FF, D_MODEL)), const((1, D_MODEL))],
        out_specs=[tok_spec, tok_spec] if final else [tok_spec],
        out_shape=[tok_shape, tok_shape] if final else [tok_shape],
        compiler_params=_cparams(("parallel",)),
        name="mix_ffn",
    )(x, ya, yb, z_rest, z_rest, mod_l, lp["norm2_g"], lp["w_out"], lp["w1"], lp["w2"], final_g)


def _block_diag2(m):
    z = jnp.zeros_like(m[0])
    return jnp.concatenate([jnp.concatenate([m[0], z], axis=1), jnp.concatenate([z, m[1]], axis=1)], axis=0)


def _layer_params(l, w_in, mu_shift, w0, w_up, a0, a_up, g_up, k_k, k_a, r_k, lnx_g, lnx_b, w_branch_a,
                  ln_v_g, w_s, b_s, w_branch_b, w_out, w1, w2, norm1_g, norm2_g):
    head_of = np.arange(D_MODEL) // HEAD
    e = (head_of[:, None] == np.arange(LANE)[None, :]).astype(np.float32)
    row = lambda a: a.reshape(1, -1)
    return dict(
        w_in_rwkv=w_in[l][:, :C_RWKV].astype(BF16), w_in_rest=w_in[l][:, C_RWKV:].astype(BF16),
        mu=row(mu_shift[l]), wup=_block_diag2(w_up[l]).astype(BF16), aup=_block_diag2(a_up[l]).astype(BF16),
        gup=g_up[l].astype(BF16), w0=row(w0[l]), a0=row(a0[l]), k_k=row(k_k[l]), k_a=row(k_a[l]),
        r_k=row(r_k[l]), lnx_g=row(lnx_g[l]), lnx_b=row(lnx_b[l]), w_branch_a=w_branch_a[l].astype(BF16),
        ln_v_g=row(ln_v_g[l]), w_s=w_s[l].astype(BF16), b_s=jnp.repeat(b_s[l].T, HEAD_B, axis=1),
        w_branch_b=w_branch_b[l].astype(BF16), w_out=w_out[l].astype(BF16), w1=w1[l].astype(BF16),
        w2=w2[l].astype(BF16), norm1_g=row(norm1_g[l]), norm2_g=row(norm2_g[l]),
        e=jnp.asarray(e, BF16), et=jnp.asarray(e.T, BF16))


def kernel(x_prompt, x_sample, state_rwkv, c, c_ctx, w_ada, b_ada, norm1_g, norm2_g, w_in, mu_shift, w0, w_up,
           a0, a_up, g_up, k_k, k_a, r_k, lnx_g, lnx_b, w_branch_a, ln_v_g, w_s, b_s, w_branch_b, w_out, w1, w2,
           final_g):
    b_ctx, t_ctx, _ = x_prompt.shape
    b_lat, t_lat, _ = x_sample.shape
    depth = w_in.shape[0]
    n_ctx = b_ctx * t_ctx
    n_lat = b_lat * t_lat
    assert t_ctx == TM_TOK and t_lat % TM_TOK == 0 and t_lat % GRID_W == 0

    x = jnp.concatenate([x_prompt.reshape(n_ctx, D_MODEL), x_sample.reshape(n_lat, D_MODEL)], axis=0)
    cond = jnp.concatenate([c_ctx[None, :], c], axis=0)
    mod = _modulation(cond, w_ada, b_ada).reshape(depth, 1 + b_lat, N_MOD, D_MODEL)
    seqs = [(0, b_ctx, t_ctx), (n_ctx, b_lat, t_lat)]
    s_zero = jnp.zeros((b_ctx, 2, N_HEAD, HEAD, HEAD), F32)
    final_row = final_g.reshape(1, D_MODEL)

    states = []
    y_final = None
    for l in range(depth):
        lp = _layer_params(l, w_in, mu_shift, w0, w_up, a0, a_up, g_up, k_k, k_a, r_k, lnx_g, lnx_b,
                           w_branch_a, ln_v_g, w_s, b_s, w_branch_b, w_out, w1, w2, norm1_g, norm2_g)
        row512 = _mod_row_map(n_ctx, t_lat, 512)
        row_tm = _mod_row_map(n_ctx, t_lat, TM_TOK)
        z_rwkv = _in_proj(x, mod[l], lp["norm1_g"], lp["w_in_rwkv"], 1152, row512)
        z_rest = _in_proj(x, mod[l], lp["norm1_g"], lp["w_in_rest"], 1024, row512)
        r, v, nk, g, bv, lw, kd, bd = _prep(z_rwkv, lp, n_ctx, t_lat)
        h0 = jnp.concatenate([s_zero, state_rwkv[:, l]], axis=0)
        y, h_fin = _scan(lw, kd, bd, r, v, nk, h0, seqs)
        states.append(h_fin[:b_ctx])
        ya = _post(y, bv, g, lp)
        yb = _cmlp(z_rest, lp)
        outs = _ffn(x, ya, yb, z_rest, mod[l], lp, final_row, row_tm, final=(l == depth - 1))
        x = outs[0]
        if l == depth - 1:
            y_final = outs[1]

    y_prompt = y_final[:n_ctx].reshape(b_ctx, t_ctx, D_MODEL)
    y_sample = y_final[n_ctx:].reshape(b_lat, t_lat, D_MODEL)
    return (y_prompt, y_sample, jnp.stack(states, axis=1))
```

```python
import functools
import math

import numpy as np
import jax
import jax.numpy as jnp
from jax import lax
from jax.experimental import pallas as pl
from jax.experimental.pallas import tpu as pltpu

F32 = jnp.float32
BF16 = jnp.bfloat16

D_MODEL = 1024
HEAD = 64
N_HEAD = D_MODEL // HEAD
LORA = 64
G_LORA = 128
C_RWKV = 3 * D_MODEL + 4 * LORA + G_LORA
D_REST = 4 * D_MODEL
D_FF = 4 * D_MODEL
GRID_W = 64
CHUNK = 128
H_B = 8
HEAD_B = D_MODEL // H_B
N_MOD = 6
EPS = 1e-6
GN_EPS = 64e-5
DECAY_SCALE = math.exp(-0.5)

SCAN_L = 64
LANE = 128
SCAN_W = 1024
SCAN_PASSES = dict(aa=1, x=1, neu=1, wu=1, gs=1, y=1, up=3)
TM_TOK = 256
VMEM_LIMIT = 56 * 1024 * 1024


def _cparams(sem):
    return pltpu.CompilerParams(dimension_semantics=sem, vmem_limit_bytes=VMEM_LIMIT)


def _split(x):
    hi = x.astype(BF16)
    lo = (x - hi.astype(F32)).astype(BF16)
    return hi, lo


def _dot(a, b, dims=(((1,), (0,)), ((), ()))):
    return lax.dot_general(a, b, dims, preferred_element_type=F32)


_NN = (((1,), (0,)), ((), ()))
_NT = (((1,), (1,)), ((), ()))
_TN = (((0,), (0,)), ((), ()))


def _dot3(a, b, dims=_NN):
    ah, al = _split(a)
    bh, bl = _split(b)
    return _dot(ah, bh, dims) + (_dot(ah, bl, dims) + _dot(al, bh, dims))


def _seg_sum(x, e_ref, et_ref):
    hi, lo = _split(x)
    s = _dot(hi, e_ref[...]) + _dot(lo, e_ref[...])
    shi, slo = _split(s)
    return _dot(shi, et_ref[...]) + _dot(slo, et_ref[...])


def _mod_kernel(c_ref, w_ref, b_ref, o_ref):
    cond = c_ref[...]
    o_ref[0] = _dot(jax.nn.silu(cond).astype(BF16), w_ref[0].astype(BF16)) + b_ref[0]


def _modulation(cond, w_ada, b_ada):
    depth = w_ada.shape[0]
    n = cond.shape[0]
    tn = 1536
    return pl.pallas_call(
        _mod_kernel,
        grid=(depth, (N_MOD * D_MODEL) // tn),
        in_specs=[pl.BlockSpec((n, D_MODEL), lambda l, j: (0, 0)),
                  pl.BlockSpec((1, D_MODEL, tn), lambda l, j: (l, 0, j)),
                  pl.BlockSpec((1, 1, tn), lambda l, j: (l, 0, j))],
        out_specs=pl.BlockSpec((1, n, tn), lambda l, j: (l, 0, j)),
        out_shape=jax.ShapeDtypeStruct((depth, n, N_MOD * D_MODEL), F32),
        compiler_params=_cparams(("parallel", "parallel")),
        name="modulation",
    )(cond, w_ada, b_ada.reshape(depth, 1, N_MOD * D_MODEL))


def _mod_row_map(n_ctx_tok, t_lat, tm):
    def row(i):
        tok = i * tm
        return jnp.where(tok < n_ctx_tok, 0, 1 + (tok - n_ctx_tok) // t_lat)
    return row


def _rms_mod(x, g, shift, scale):
    y = x * lax.rsqrt(jnp.mean(x * x, axis=-1, keepdims=True) + EPS) * g
    return y * (1.0 + scale) + shift


def _in_proj_kernel(x_ref, mod_ref, g_ref, w_ref, o_ref, h_scr):
    @pl.when(pl.program_id(1) == 0)
    def _():
        m = mod_ref[0]
        h_scr[...] = _rms_mod(x_ref[...], g_ref[...], m[0:1], m[1:2]).astype(BF16)

    o_ref[...] = _dot(h_scr[...], w_ref[...])


def _in_proj(x, mod_l, g, w, tn, mod_row, tm=512):
    n_tok = x.shape[0]
    n_out = w.shape[1]
    return pl.pallas_call(
        _in_proj_kernel,
        grid=(n_tok // tm, n_out // tn),
        in_specs=[pl.BlockSpec((tm, D_MODEL), lambda i, j: (i, 0)),
                  pl.BlockSpec((1, N_MOD, D_MODEL), lambda i, j: (mod_row(i), 0, 0)),
                  pl.BlockSpec((1, D_MODEL), lambda i, j: (0, 0)),
                  pl.BlockSpec((D_MODEL, tn), lambda i, j: (0, j))],
        out_specs=pl.BlockSpec((tm, tn), lambda i, j: (i, j)),
        out_shape=jax.ShapeDtypeStruct((n_tok, n_out), F32),
        scratch_shapes=[pltpu.VMEM((tm, D_MODEL), BF16)],
        compiler_params=_cparams(("parallel", "arbitrary")),
        name="in_proj",
    )(x, mod_l, g, w)


def _shift_rows(z, k, fill_first):
    return jnp.concatenate([fill_first, z[: z.shape[0] - k]], axis=0)


def _prep_kernel(z_ref, zp_ref, zn_ref, mu_ref, wup_ref, aup_ref, gup_ref, w0_ref, a0_ref, kk_ref, ka_ref,
                 rk_ref, e_ref, et_ref,
                 r_ref, v_ref, nk_ref, g_ref, bv_ref, lw_ref, kd_ref, bd_ref, zs_scr,
                 *, n_ctx_tiles, lat_tiles_per_seq):
    i = pl.program_id(0)
    tm = z_ref.shape[0]
    z = z_ref[...]
    row = lax.broadcasted_iota(jnp.int32, z.shape, 0)
    lane = lax.broadcasted_iota(jnp.int32, z.shape, 1)
    prev1 = jnp.where(row == 0, 0.0, pltpu.roll(z, 1, 0))
    next1 = jnp.where(row == tm - 1, 0.0, pltpu.roll(z, tm - 1, 0))

    @pl.when(i < n_ctx_tiles)
    def _():
        zs_scr[...] = jnp.where(lane % 2 == 0, prev1, next1)

    @pl.when(i >= n_ctx_tiles)
    def _():
        j = (i - n_ctx_tiles) % lat_tiles_per_seq
        col = row % GRID_W
        left = jnp.where(col == 0, 0.0, prev1)
        right = jnp.where(col == GRID_W - 1, 0.0, next1)
        up_halo = jnp.where(j == 0, 0.0, zp_ref[...])
        dn_halo = jnp.where(j == lat_tiles_per_seq - 1, 0.0, zn_ref[...])
        up = jnp.concatenate([up_halo, z[: tm - GRID_W]], axis=0)
        down = jnp.concatenate([z[GRID_W:], dn_halo], axis=0)
        m = lane % 4
        zs_scr[...] = jnp.where(m == 0, left, jnp.where(m == 1, right, jnp.where(m == 2, up, down)))

    zs = z + mu_ref[...] * (zs_scr[...] - z)
    d = D_MODEL
    r = zs[:, 0:d]
    k = zs[:, d:2 * d]
    v = zs[:, 2 * d:3 * d]
    wd = zs[:, 3 * d:3 * d + 2 * LORA]
    ad = zs[:, 3 * d + 2 * LORA:3 * d + 4 * LORA]
    gd = zs[:, 3 * d + 4 * LORA:]
    r_ref[...] = r
    v_ref[...] = v
    g_ref[...] = _dot(jax.nn.sigmoid(gd).astype(BF16), gup_ref[...])
    w_logit = w0_ref[...] + _dot(jnp.tanh(wd).astype(BF16), wup_ref[...])
    a_all = jax.nn.sigmoid(a0_ref[...] + _dot(ad.astype(BF16), aup_ref[...]))
    kk = k * kk_ref[...]
    kk = kk * lax.rsqrt(_seg_sum(kk * kk, e_ref, et_ref) + 1e-12)
    nk_ref[...] = -kk
    ka = ka_ref[...]
    kd_sum = None
    for dr in range(2):
        a = a_all[:, dr * d:(dr + 1) * d]
        lw_ref[dr] = -DECAY_SCALE * jax.nn.sigmoid(w_logit[:, dr * d:(dr + 1) * d])
        kd = k * (1.0 + (a - 1.0) * ka)
        kd_ref[dr] = kd
        bd_ref[dr] = a * kk
        kd_sum = kd if kd_sum is None else kd_sum + kd
    bv_ref[...] = _seg_sum(kd_sum * rk_ref[...] * r, e_ref, et_ref) * v


def _prep(z_rwkv, lp, n_ctx_tok, t_lat):
    n_tok = z_rwkv.shape[0]
    tm = TM_TOK
    hb = tm // GRID_W
    n_hblk = n_tok // GRID_W
    tok_spec = pl.BlockSpec((tm, D_MODEL), lambda i: (i, 0))
    dir_spec = pl.BlockSpec((2, tm, D_MODEL), lambda i: (0, i, 0))

    def const(shape):
        return pl.BlockSpec(shape, lambda i: (0,) * len(shape))

    kern = functools.partial(_prep_kernel, n_ctx_tiles=n_ctx_tok // tm, lat_tiles_per_seq=t_lat // tm)
    tok_shape = jax.ShapeDtypeStruct((n_tok, D_MODEL), F32)
    dir_shape = jax.ShapeDtypeStruct((2, n_tok, D_MODEL), F32)
    return pl.pallas_call(
        kern,
        grid=(n_tok // tm,),
        in_specs=[pl.BlockSpec((tm, C_RWKV), lambda i: (i, 0)),
                  pl.BlockSpec((GRID_W, C_RWKV), lambda i: (jnp.maximum(i * hb - 1, 0), 0)),
                  pl.BlockSpec((GRID_W, C_RWKV), lambda i: (jnp.minimum((i + 1) * hb, n_hblk - 1), 0)),
                  const((1, C_RWKV)), const((2 * LORA, 2 * D_MODEL)), const((2 * LORA, 2 * D_MODEL)),
                  const((G_LORA, D_MODEL)), const((1, 2 * D_MODEL)), const((1, 2 * D_MODEL)),
                  const((1, D_MODEL)), const((1, D_MODEL)), const((1, D_MODEL)),
                  const((D_MODEL, LANE)), const((LANE, D_MODEL))],
        out_specs=[tok_spec, tok_spec, tok_spec, tok_spec, tok_spec, dir_spec, dir_spec, dir_spec],
        out_shape=[tok_shape] * 5 + [dir_shape] * 3,
        scratch_shapes=[pltpu.VMEM((tm, C_RWKV), F32)],
        compiler_params=_cparams(("parallel",)),
        name="rwkv_prep",
    )(z_rwkv, z_rwkv, z_rwkv, lp["mu"], lp["wup"], lp["aup"], lp["gup"], lp["w0"], lp["a0"], lp["k_k"],
      lp["k_a"], lp["r_k"], lp["e"], lp["et"])


def _mm(a, b, dims=_NN, passes=3):
    out = _dot(a[0], b[0], dims)
    if passes == 3:
        out = out + (_dot(a[0], b[1], dims) + _dot(a[1], b[0], dims))
    return out


def _cols(p, sl):
    return (p[0][:, sl], p[1][:, sl])


def _rows(p, sl):
    return (p[0][sl], p[1][sl])


def _cat(ps, axis):
    return (jnp.concatenate([p[0] for p in ps], axis=axis), jnp.concatenate([p[1] for p in ps], axis=axis))


def _scan_kernel(blk_ref, first_ref, seq_ref, lw_ref, kd_ref, bd_ref, r_ref, v_ref, nk_ref, h0_ref,
                 y_ref, hT_ref, s_scr):
    del blk_ref, seq_ref
    dr = pl.program_id(0)
    s = pl.program_id(2)
    n = lw_ref.shape[1]
    heads = range(lw_ref.shape[2] // HEAD)
    hs = [slice(h * HEAD, (h + 1) * HEAD) for h in heads]
    lo_half = slice(0, n)
    hi_half = slice(n, 2 * n)

    @pl.when(first_ref[s] == 1)
    def _():
        s_scr[...] = h0_ref[0, 0]

    sign = 1 - 2 * dr
    row = lax.broadcasted_iota(jnp.int32, (n, n), 0)
    col = lax.broadcasted_iota(jnp.int32, (n, n), 1)
    tri = ((row - col) * sign >= 0).astype(BF16)
    row2 = lax.broadcasted_iota(jnp.int32, (n, 2 * n), 0)
    lane2 = lax.broadcasted_iota(jnp.int32, (n, 2 * n), 1)
    left = lane2 < n
    diff2 = (row2 - jnp.where(left, lane2, lane2 - n)) * sign
    incl2 = diff2 >= 0
    strict2 = diff2 > 0
    eye_right = jnp.where(jnp.logical_and(jnp.logical_not(left), diff2 == 0), 1.0, 0.0)

    lw = lw_ref[0]
    lhi, llo = _split(lw)
    cum = _dot(tri, lhi) + _dot(tri, llo)
    tot = jnp.where(dr == 0, cum[n - 1:n, :], cum[0:1, :])
    g_inv = jnp.exp(-cum)
    g_rest = jnp.exp(tot - cum)
    g_tot = jnp.exp(tot)
    kd = kd_ref[0]
    bd = bd_ref[0]
    v = v_ref[...]
    nr = _split(jnp.concatenate([nk_ref[...] * jnp.exp(cum - lw), r_ref[...] * jnp.exp(cum)], axis=0))
    bk = _split(jnp.concatenate([bd * g_inv, kd * g_inv], axis=0))
    bkh = _split(jnp.concatenate([bd * g_rest, kd * g_rest], axis=0))
    vs = _split(v)

    ps = SCAN_PASSES
    aa = [_mm(_cols(nr, c), _cols(bk, c), _NT, ps["aa"]) for c in hs]
    top = [jnp.where(strict2, a[:n], 0.0) for a in aa]
    bot = [_split(jnp.where(incl2, a[n:], 0.0)) for a in aa]
    tops = [_split(t) for t in top]
    x = [_mm(_cols(tp, hi_half), _cols(vs, c), _NN, ps["x"]) for tp, c in zip(tops, hs)]
    slab = [jnp.where(left, t, eye_right) for t in top]
    m = 1
    while m < n:
        sp = [_split(sb) for sb in slab]
        slab = [_mm(_cols(p, lo_half), p, _NN, ps["neu"]) + jnp.where(left, 0.0, sb) for p, sb in zip(sp, slab)]
        m *= 2
    sp = [_split(sb) for sb in slab]
    rhs = [_cat([_rows(_cols(nr, c), lo_half), _split(xh)], axis=1) for c, xh in zip(hs, x)]
    wu = [_mm(_cols(p, hi_half), q, _NN, ps["wu"]) for p, q in zip(sp, rhs)]
    st = [s_scr[h] for h in heads]
    gs = [_mm(_cat([_cols(_split(w), lo_half), _rows(_cols(nr, c), hi_half)], axis=0), _split(sh), _NT, ps["gs"])
          for w, c, sh in zip(wu, hs, st)]
    uv = [_split(jnp.concatenate([g[:n] + w[:, hi_half], v[:, c]], axis=0)) for g, w, c in zip(gs, wu, hs)]
    for h in heads:
        y_ref[0, :, hs[h]] = gs[h][n:] + _mm(bot[h], uv[h], _NN, ps["y"])
    for h in heads:
        s_scr[h] = st[h] * g_tot[:, hs[h]] + _mm(uv[h], _cols(bkh, hs[h]), _TN, ps["up"])

    hT_ref[0, 0] = s_scr[...]


def _scan_tables(seqs):
    blk = [[], []]
    first, seq = [], []
    sid = 0
    for tok0, nb, t in seqs:
        nc = t // SCAN_L
        for b in range(nb):
            base = (tok0 + b * t) // SCAN_L
            for c in range(nc):
                blk[0].append(base + c)
                blk[1].append(base + nc - 1 - c)
                first.append(1 if c == 0 else 0)
                seq.append(sid)
            sid += 1
    return (jnp.asarray(np.array(blk, np.int32).reshape(-1)), jnp.asarray(np.array(first, np.int32)),
            jnp.asarray(np.array(seq, np.int32)), len(first))


def _scan(lw, kd, bd, r, v, nk, h0, seqs):
    n_tok = r.shape[0]
    n_seq = h0.shape[0]
    blk, first, seq, n_steps = _scan_tables(seqs)
    hp = SCAN_W // HEAD
    n_pair = N_HEAD // hp

    def dir_map(d, p, s, blk, first, seq):
        return (d, blk[d * n_steps + s], p)

    def tok_map(d, p, s, blk, first, seq):
        return (blk[d * n_steps + s], p)

    def st_map(d, p, s, blk, first, seq):
        return (seq[s], d, p, 0, 0)

    dir_spec = pl.BlockSpec((1, SCAN_L, SCAN_W), dir_map)
    tok_spec = pl.BlockSpec((SCAN_L, SCAN_W), tok_map)
    st_spec = pl.BlockSpec((1, 1, hp, HEAD, HEAD), st_map)
    return pl.pallas_call(
        _scan_kernel,
        grid_spec=pltpu.PrefetchScalarGridSpec(
            num_scalar_prefetch=3,
            grid=(2, n_pair, n_steps),
            in_specs=[dir_spec, dir_spec, dir_spec, tok_spec, tok_spec, tok_spec, st_spec],
            out_specs=[dir_spec, st_spec],
            scratch_shapes=[pltpu.VMEM((hp, HEAD, HEAD), F32)]),
        out_shape=[jax.ShapeDtypeStruct((2, n_tok, D_MODEL), F32),
                   jax.ShapeDtypeStruct((n_seq, 2, N_HEAD, HEAD, HEAD), F32)],
        compiler_params=_cparams(("parallel", "parallel", "arbitrary")),
        name="rwkv_scan",
    )(blk, first, seq, lw, kd, bd, r, v, nk, h0)


def _post_kernel(y0_ref, y1_ref, bv_ref, g_ref, lg_ref, lb_ref, e_ref, et_ref, w_ref, o_ref):
    y = y0_ref[0] + y1_ref[0]
    mu = _seg_sum(y, e_ref, et_ref) * (1.0 / HEAD)
    yc = y - mu
    var = _seg_sum(yc * yc, e_ref, et_ref) * (1.0 / HEAD)
    yn = yc * lax.rsqrt(var + GN_EPS) * lg_ref[...] + lb_ref[...]
    o_ref[...] = _dot(((yn + bv_ref[...]) * g_ref[...]).astype(BF16), w_ref[...])


def _post(y, bv, g, lp):
    n_tok = bv.shape[0]
    tm = TM_TOK
    tok_spec = pl.BlockSpec((tm, D_MODEL), lambda i: (i, 0))

    def const(shape):
        return pl.BlockSpec(shape, lambda i: (0,) * len(shape))

    return pl.pallas_call(
        _post_kernel,
        grid=(n_tok // tm,),
        in_specs=[pl.BlockSpec((1, tm, D_MODEL), lambda i: (0, i, 0)),
                  pl.BlockSpec((1, tm, D_MODEL), lambda i: (1, i, 0)),
                  tok_spec, tok_spec, const((1, D_MODEL)), const((1, D_MODEL)),
                  const((D_MODEL, LANE)), const((LANE, D_MODEL)), const((D_MODEL, D_MODEL))],
        out_specs=tok_spec,
        out_shape=jax.ShapeDtypeStruct((n_tok, D_MODEL), F32),
        compiler_params=_cparams(("parallel",)),
        name="rwkv_post",
    )(y, y, bv, g, lp["lnx_g"], lp["lnx_b"], lp["e"], lp["et"], lp["w_branch_a"])


def _cmlp_kernel(zu_ref, zv_ref, lg_ref, ws_ref, bs_ref, w_ref, o_ref, y_scr):
    v = jax.nn.gelu(zv_ref[...])
    mu = jnp.mean(v, axis=-1, keepdims=True)
    vc = v - mu
    var = jnp.mean(vc * vc, axis=-1, keepdims=True)
    vn = (vc * lax.rsqrt(var + EPS) * lg_ref[...]).astype(BF16)
    u = jax.nn.gelu(zu_ref[...])
    for c in range(zu_ref.shape[0] // CHUNK):
        rows = slice(c * CHUNK, (c + 1) * CHUNK)
        for h in range(H_B):
            cols = slice(h * HEAD_B, (h + 1) * HEAD_B)
            s = _dot(ws_ref[h], vn[rows, cols]) + bs_ref[:, cols]
            y_scr[rows, cols] = (u[rows, cols] * s).astype(BF16)
    o_ref[...] = _dot(y_scr[...], w_ref[...])


def _cmlp(z_rest, lp):
    n_tok = z_rest.shape[0]
    tm = TM_TOK

    def const(shape):
        return pl.BlockSpec(shape, lambda i: (0,) * len(shape))

    return pl.pallas_call(
        _cmlp_kernel,
        grid=(n_tok // tm,),
        in_specs=[pl.BlockSpec((tm, D_MODEL), lambda i: (i, 0)),
                  pl.BlockSpec((tm, D_MODEL), lambda i: (i, 1)),
                  const((1, D_MODEL)), const((H_B, CHUNK, CHUNK)), const((CHUNK, D_MODEL)),
                  const((D_MODEL, D_MODEL))],
        out_specs=pl.BlockSpec((tm, D_MODEL), lambda i: (i, 0)),
        out_shape=jax.ShapeDtypeStruct((n_tok, D_MODEL), F32),
        scratch_shapes=[pltpu.VMEM((tm, D_MODEL), BF16)],
        compiler_params=_cparams(("parallel",)),
        name="chunk_mlp",
    )(z_rest, z_rest, lp["ln_v_g"], lp["w_s"], lp["b_s"], lp["w_branch_b"])


def _ffn_kernel(x_ref, ya_ref, yb_ref, ga_ref, gb_ref, mod_ref, g2_ref, wo_ref, w1_ref, w2_ref, fg_ref,
                o_ref, *maybe_final, final):
    m = mod_ref[0]
    mixed = jax.nn.sigmoid(ga_ref[...]) * ya_ref[...] + jax.nn.sigmoid(gb_ref[...]) * yb_ref[...]
    x = x_ref[...] + m[2:3] * _dot(mixed.astype(BF16), wo_ref[...])
    h2 = _rms_mod(x, g2_ref[...], m[3:4], m[4:5]).astype(BF16)
    acc = jnp.zeros(x.shape, F32)
    ff_chunk = D_MODEL
    for c in range(D_FF // ff_chunk):
        cols = slice(c * ff_chunk, (c + 1) * ff_chunk)
        hid = jnp.square(jnp.maximum(_dot(h2, w1_ref[:, cols]), 0.0)).astype(BF16)
        acc = acc + _dot(hid, w2_ref[cols, :])
    x = x + m[5:6] * acc
    o_ref[...] = x
    if final:
        yf = x * lax.rsqrt(jnp.mean(x * x, axis=-1, keepdims=True) + EPS) * fg_ref[...]
        maybe_final[0][...] = yf


def _ffn(x, ya, yb, z_rest, mod_l, lp, final_g, mod_row, final):
    n_tok = x.shape[0]
    tm = TM_TOK
    tok_spec = pl.BlockSpec((tm, D_MODEL), lambda i: (i, 0))

    def const(shape):
        return pl.BlockSpec(shape, lambda i: (0,) * len(shape), pipeline_mode=pl.Buffered(1))

    tok_shape = jax.ShapeDtypeStruct((n_tok, D_MODEL), F32)
    return pl.pallas_call(
        functools.partial(_ffn_kernel, final=final),
        grid=(n_tok // tm,),
        in_specs=[tok_spec, tok_spec, tok_spec,
                  pl.BlockSpec((tm, D_MODEL), lambda i: (i, 2)),
                  pl.BlockSpec((tm, D_MODEL), lambda i: (i, 3)),
                  pl.BlockSpec((1, N_MOD, D_MODEL), lambda i: (mod_row(i), 0, 0)),
                  const((1, D_MODEL)), const((D_MODEL, D_MODEL)), const((D_MODEL, D_FF)),
                  const((D_FF, D_MODEL)), const((1, D_MODEL))],
        out_specs=[tok_spec, tok_spec] if final else [tok_spec],
        out_shape=[tok_shape, tok_shape] if final else [tok_shape],
        compiler_params=_cparams(("parallel",)),
        name="mix_ffn",
    )(x, ya, yb, z_rest, z_rest, mod_l, lp["norm2_g"], lp["w_out"], lp["w1"], lp["w2"], final_g)


def _block_diag2(m):
    z = jnp.zeros_like(m[0])
    return jnp.concatenate([jnp.concatenate([m[0], z], axis=1), jnp.concatenate([z, m[1]], axis=1)], axis=0)


def _layer_params(l, w_in, mu_shift, w0, w_up, a0, a_up, g_up, k_k, k_a, r_k, lnx_g, lnx_b, w_branch_a,
                  ln_v_g, w_s, b_s, w_branch_b, w_out, w1, w2, norm1_g, norm2_g):
    head_of = np.arange(D_MODEL) // HEAD
    e = (head_of[:, None] == np.arange(LANE)[None, :]).astype(np.float32)
    row = lambda a: a.reshape(1, -1)
    return dict(
        w_in_rwkv=w_in[l][:, :C_RWKV].astype(BF16), w_in_rest=w_in[l][:, C_RWKV:].astype(BF16),
        mu=row(mu_shift[l]), wup=_block_diag2(w_up[l]).astype(BF16), aup=_block_diag2(a_up[l]).astype(BF16),
        gup=g_up[l].astype(BF16), w0=row(w0[l]), a0=row(a0[l]), k_k=row(k_k[l]), k_a=row(k_a[l]),
        r_k=row(r_k[l]), lnx_g=row(lnx_g[l]), lnx_b=row(lnx_b[l]), w_branch_a=w_branch_a[l].astype(BF16),
        ln_v_g=row(ln_v_g[l]), w_s=w_s[l].astype(BF16), b_s=jnp.repeat(b_s[l].T, HEAD_B, axis=1),
        w_branch_b=w_branch_b[l].astype(BF16), w_out=w_out[l].astype(BF16), w1=w1[l].astype(BF16),
        w2=w2[l].astype(BF16), norm1_g=row(norm1_g[l]), norm2_g=row(norm2_g[l]),
        e=jnp.asarray(e, BF16), et=jnp.asarray(e.T, BF16))


def kernel(x_prompt, x_sample, state_rwkv, c, c_ctx, w_ada, b_ada, norm1_g, norm2_g, w_in, mu_shift, w0, w_up,
           a0, a_up, g_up, k_k, k_a, r_k, lnx_g, lnx_b, w_branch_a, ln_v_g, w_s, b_s, w_branch_b, w_out, w1, w2,
           final_g):
    b_ctx, t_ctx, _ = x_prompt.shape
    b_lat, t_lat, _ = x_sample.shape
    depth = w_in.shape[0]
    n_ctx = b_ctx * t_ctx
    n_lat = b_lat * t_lat
    assert t_ctx == TM_TOK and t_lat % TM_TOK == 0 and t_lat % GRID_W == 0

    x = jnp.concatenate([x_prompt.reshape(n_ctx, D_MODEL), x_sample.reshape(n_lat, D_MODEL)], axis=0)
    cond = jnp.concatenate([c_ctx[None, :], c], axis=0)
    mod = _modulation(cond, w_ada, b_ada).reshape(depth, 1 + b_lat, N_MOD, D_MODEL)
    seqs = [(0, b_ctx, t_ctx), (n_ctx, b_lat, t_lat)]
    s_zero = jnp.zeros((b_ctx, 2, N_HEAD, HEAD, HEAD), F32)
    final_row = final_g.reshape(1, D_MODEL)

    states = []
    y_final = None
    for l in range(depth):
        lp = _layer_params(l, w_in, mu_shift, w0, w_up, a0, a_up, g_up, k_k, k_a, r_k, lnx_g, lnx_b,
                           w_branch_a, ln_v_g, w_s, b_s, w_branch_b, w_out, w1, w2, norm1_g, norm2_g)
        row512 = _mod_row_map(n_ctx, t_lat, 512)
        row_tm = _mod_row_map(n_ctx, t_lat, TM_TOK)
        z_rwkv = _in_proj(x, mod[l], lp["norm1_g"], lp["w_in_rwkv"], 1152, row512)
        z_rest = _in_proj(x, mod[l], lp["norm1_g"], lp["w_in_rest"], 1024, row512)
        r, v, nk, g, bv, lw, kd, bd = _prep(z_rwkv, lp, n_ctx, t_lat)
        h0 = jnp.concatenate([s_zero, state_rwkv[:, l]], axis=0)
        y, h_fin = _scan(lw, kd, bd, r, v, nk, h0, seqs)
        states.append(h_fin[:b_ctx])
        ya = _post(y, bv, g, lp)
        yb = _cmlp(z_rest, lp)
        outs = _ffn(x, ya, yb, z_rest, mod[l], lp, final_row, row_tm, final=(l == depth - 1))
        x = outs[0]
        if l == depth - 1:
            y_final = outs[1]

    y_prompt = y_final[:n_ctx].reshape(b_ctx, t_ctx, D_MODEL)
    y_sample = y_final[n_ctx:].reshape(b_lat, t_lat, D_MODEL)
    return (y_prompt, y_sample, jnp.stack(states, axis=1))
```

```python
import functools
import math

import numpy as np
import jax
import jax.numpy as jnp
from jax import lax
from jax.experimental import pallas as pl
from jax.experimental.pallas import tpu as pltpu

F32 = jnp.float32
BF16 = jnp.bfloat16

D_MODEL = 1024
HEAD = 64
N_HEAD = D_MODEL // HEAD
LORA = 64
G_LORA = 128
C_RWKV = 3 * D_MODEL + 4 * LORA + G_LORA
D_REST = 4 * D_MODEL
D_FF = 4 * D_MODEL
GRID_W = 64
CHUNK = 128
H_B = 8
HEAD_B = D_MODEL // H_B
N_MOD = 6
EPS = 1e-6
GN_EPS = 64e-5
DECAY_SCALE = math.exp(-0.5)

SCAN_L = 64
LANE = 128
SCAN_W = 1024
SCAN_PASSES = dict(aa=1, x=1, neu=1, wu=1, gs=1, y=1, up=1)
INV_BASE = 8
TM_TOK = 256
VMEM_LIMIT = 56 * 1024 * 1024


def _cparams(sem):
    return pltpu.CompilerParams(dimension_semantics=sem, vmem_limit_bytes=VMEM_LIMIT)


def _split(x):
    hi = x.astype(BF16)
    lo = (x - hi.astype(F32)).astype(BF16)
    return hi, lo


def _dot(a, b, dims=(((1,), (0,)), ((), ()))):
    return lax.dot_general(a, b, dims, preferred_element_type=F32)


_NN = (((1,), (0,)), ((), ()))
_NT = (((1,), (1,)), ((), ()))
_TN = (((0,), (0,)), ((), ()))


def _dot3(a, b, dims=_NN):
    ah, al = _split(a)
    bh, bl = _split(b)
    return _dot(ah, bh, dims) + (_dot(ah, bl, dims) + _dot(al, bh, dims))


def _seg_sum(x, e_ref, et_ref):
    hi, lo = _split(x)
    s = _dot(hi, e_ref[...]) + _dot(lo, e_ref[...])
    shi, slo = _split(s)
    return _dot(shi, et_ref[...]) + _dot(slo, et_ref[...])


def _mod_kernel(c_ref, w_ref, b_ref, o_ref):
    cond = c_ref[...]
    o_ref[0] = _dot(jax.nn.silu(cond).astype(BF16), w_ref[0].astype(BF16)) + b_ref[0]


def _modulation(cond, w_ada, b_ada):
    depth = w_ada.shape[0]
    n = cond.shape[0]
    tn = 1536
    return pl.pallas_call(
        _mod_kernel,
        grid=(depth, (N_MOD * D_MODEL) // tn),
        in_specs=[pl.BlockSpec((n, D_MODEL), lambda l, j: (0, 0)),
                  pl.BlockSpec((1, D_MODEL, tn), lambda l, j: (l, 0, j)),
                  pl.BlockSpec((1, 1, tn), lambda l, j: (l, 0, j))],
        out_specs=pl.BlockSpec((1, n, tn), lambda l, j: (l, 0, j)),
        out_shape=jax.ShapeDtypeStruct((depth, n, N_MOD * D_MODEL), F32),
        compiler_params=_cparams(("parallel", "parallel")),
        name="modulation",
    )(cond, w_ada, b_ada.reshape(depth, 1, N_MOD * D_MODEL))


def _mod_row_map(n_ctx_tok, t_lat, tm):
    def row(i):
        tok = i * tm
        return jnp.where(tok < n_ctx_tok, 0, 1 + (tok - n_ctx_tok) // t_lat)
    return row


def _rms_mod(x, g, shift, scale):
    y = x * lax.rsqrt(jnp.mean(x * x, axis=-1, keepdims=True) + EPS) * g
    return y * (1.0 + scale) + shift


def _in_proj_kernel(x_ref, mod_ref, g_ref, w_ref, o_ref, h_scr):
    @pl.when(pl.program_id(1) == 0)
    def _():
        m = mod_ref[0]
        h_scr[...] = _rms_mod(x_ref[...], g_ref[...], m[0:1], m[1:2]).astype(BF16)

    o_ref[...] = _dot(h_scr[...], w_ref[...])


def _in_proj(x, mod_l, g, w, tn, mod_row, tm=512):
    n_tok = x.shape[0]
    n_out = w.shape[1]
    return pl.pallas_call(
        _in_proj_kernel,
        grid=(n_tok // tm, n_out // tn),
        in_specs=[pl.BlockSpec((tm, D_MODEL), lambda i, j: (i, 0)),
                  pl.BlockSpec((1, N_MOD, D_MODEL), lambda i, j: (mod_row(i), 0, 0)),
                  pl.BlockSpec((1, D_MODEL), lambda i, j: (0, 0)),
                  pl.BlockSpec((D_MODEL, tn), lambda i, j: (0, j))],
        out_specs=pl.BlockSpec((tm, tn), lambda i, j: (i, j)),
        out_shape=jax.ShapeDtypeStruct((n_tok, n_out), F32),
        scratch_shapes=[pltpu.VMEM((tm, D_MODEL), BF16)],
        compiler_params=_cparams(("parallel", "arbitrary")),
        name="in_proj",
    )(x, mod_l, g, w)


def _shift_rows(z, k, fill_first):
    return jnp.concatenate([fill_first, z[: z.shape[0] - k]], axis=0)


def _prep_kernel(z_ref, zp_ref, zn_ref, mu_ref, wup_ref, aup_ref, gup_ref, w0_ref, a0_ref, kk_ref, ka_ref,
                 rk_ref, e_ref, et_ref,
                 r_ref, v_ref, nk_ref, g_ref, bv_ref, lw_ref, kd_ref, bd_ref, zs_scr,
                 *, n_ctx_tiles, lat_tiles_per_seq):
    i = pl.program_id(0)
    tm = z_ref.shape[0]
    z = z_ref[...]
    row = lax.broadcasted_iota(jnp.int32, z.shape, 0)
    lane = lax.broadcasted_iota(jnp.int32, z.shape, 1)
    prev1 = jnp.where(row == 0, 0.0, pltpu.roll(z, 1, 0))
    next1 = jnp.where(row == tm - 1, 0.0, pltpu.roll(z, tm - 1, 0))

    @pl.when(i < n_ctx_tiles)
    def _():
        zs_scr[...] = jnp.where(lane % 2 == 0, prev1, next1)

    @pl.when(i >= n_ctx_tiles)
    def _():
        j = (i - n_ctx_tiles) % lat_tiles_per_seq
        col = row % GRID_W
        left = jnp.where(col == 0, 0.0, prev1)
        right = jnp.where(col == GRID_W - 1, 0.0, next1)
        up_halo = jnp.where(j == 0, 0.0, zp_ref[...])
        dn_halo = jnp.where(j == lat_tiles_per_seq - 1, 0.0, zn_ref[...])
        up = jnp.concatenate([up_halo, z[: tm - GRID_W]], axis=0)
        down = jnp.concatenate([z[GRID_W:], dn_halo], axis=0)
        m = lane % 4
        zs_scr[...] = jnp.where(m == 0, left, jnp.where(m == 1, right, jnp.where(m == 2, up, down)))

    zs = z + mu_ref[...] * (zs_scr[...] - z)
    d = D_MODEL
    r = zs[:, 0:d]
    k = zs[:, d:2 * d]
    v = zs[:, 2 * d:3 * d]
    wd = zs[:, 3 * d:3 * d + 2 * LORA]
    ad = zs[:, 3 * d + 2 * LORA:3 * d + 4 * LORA]
    gd = zs[:, 3 * d + 4 * LORA:]
    r_ref[...] = r
    v_ref[...] = v
    g_ref[...] = _dot(jax.nn.sigmoid(gd).astype(BF16), gup_ref[...])
    w_logit = w0_ref[...] + _dot(jnp.tanh(wd).astype(BF16), wup_ref[...])
    a_all = jax.nn.sigmoid(a0_ref[...] + _dot(ad.astype(BF16), aup_ref[...]))
    kk = k * kk_ref[...]
    kk = kk * lax.rsqrt(_seg_sum(kk * kk, e_ref, et_ref) + 1e-12)
    nk_ref[...] = -kk
    ka = ka_ref[...]
    kd_sum = None
    for dr in range(2):
        a = a_all[:, dr * d:(dr + 1) * d]
        lw_ref[dr] = -DECAY_SCALE * jax.nn.sigmoid(w_logit[:, dr * d:(dr + 1) * d])
        kd = k * (1.0 + (a - 1.0) * ka)
        kd_ref[dr] = kd
        bd_ref[dr] = a * kk
        kd_sum = kd if kd_sum is None else kd_sum + kd
    bv_ref[...] = _seg_sum(kd_sum * rk_ref[...] * r, e_ref, et_ref) * v


def _prep(z_rwkv, lp, n_ctx_tok, t_lat):
    n_tok = z_rwkv.shape[0]
    tm = TM_TOK
    hb = tm // GRID_W
    n_hblk = n_tok // GRID_W
    tok_spec = pl.BlockSpec((tm, D_MODEL), lambda i: (i, 0))
    dir_spec = pl.BlockSpec((2, tm, D_MODEL), lambda i: (0, i, 0))

    def const(shape):
        return pl.BlockSpec(shape, lambda i: (0,) * len(shape))

    kern = functools.partial(_prep_kernel, n_ctx_tiles=n_ctx_tok // tm, lat_tiles_per_seq=t_lat // tm)
    tok_shape = jax.ShapeDtypeStruct((n_tok, D_MODEL), F32)
    dir_shape = jax.ShapeDtypeStruct((2, n_tok, D_MODEL), F32)
    return pl.pallas_call(
        kern,
        grid=(n_tok // tm,),
        in_specs=[pl.BlockSpec((tm, C_RWKV), lambda i: (i, 0)),
                  pl.BlockSpec((GRID_W, C_RWKV), lambda i: (jnp.maximum(i * hb - 1, 0), 0)),
                  pl.BlockSpec((GRID_W, C_RWKV), lambda i: (jnp.minimum((i + 1) * hb, n_hblk - 1), 0)),
                  const((1, C_RWKV)), const((2 * LORA, 2 * D_MODEL)), const((2 * LORA, 2 * D_MODEL)),
                  const((G_LORA, D_MODEL)), const((1, 2 * D_MODEL)), const((1, 2 * D_MODEL)),
                  const((1, D_MODEL)), const((1, D_MODEL)), const((1, D_MODEL)),
                  const((D_MODEL, LANE)), const((LANE, D_MODEL))],
        out_specs=[tok_spec, tok_spec, tok_spec, tok_spec, tok_spec, dir_spec, dir_spec, dir_spec],
        out_shape=[tok_shape] * 5 + [dir_shape] * 3,
        scratch_shapes=[pltpu.VMEM((tm, C_RWKV), F32)],
        compiler_params=_cparams(("parallel",)),
        name="rwkv_prep",
    )(z_rwkv, z_rwkv, z_rwkv, lp["mu"], lp["wup"], lp["aup"], lp["gup"], lp["w0"], lp["a0"], lp["k_k"],
      lp["k_a"], lp["r_k"], lp["e"], lp["et"])


def _mm(a, b, dims=_NN, passes=3):
    out = _dot(a[0], b[0], dims)
    if passes == 3:
        out = out + (_dot(a[0], b[1], dims) + _dot(a[1], b[0], dims))
    return out


def _cols(p, sl):
    return (p[0][:, sl], p[1][:, sl])


def _rows(p, sl):
    return (p[0][sl], p[1][sl])


def _cat(ps, axis):
    return (jnp.concatenate([p[0] for p in ps], axis=axis), jnp.concatenate([p[1] for p in ps], axis=axis))


def _scan_kernel(blk_ref, first_ref, seq_ref, lw_ref, kd_ref, bd_ref, r_ref, v_ref, nk_ref, h0_ref,
                 y_ref, hT_ref, s_scr):
    del blk_ref, seq_ref
    dr = pl.program_id(0)
    s = pl.program_id(2)
    n = lw_ref.shape[1]
    heads = range(lw_ref.shape[2] // HEAD)
    hs = [slice(h * HEAD, (h + 1) * HEAD) for h in heads]
    lo_half = slice(0, n)
    hi_half = slice(n, 2 * n)

    @pl.when(first_ref[s] == 1)
    def _():
        s_scr[...] = h0_ref[0, 0]

    sign = 1 - 2 * dr
    row = lax.broadcasted_iota(jnp.int32, (n, n), 0)
    col = lax.broadcasted_iota(jnp.int32, (n, n), 1)
    tri = ((row - col) * sign >= 0).astype(BF16)
    row2 = lax.broadcasted_iota(jnp.int32, (n, 2 * n), 0)
    lane2 = lax.broadcasted_iota(jnp.int32, (n, 2 * n), 1)
    left = lane2 < n
    diff2 = (row2 - jnp.where(left, lane2, lane2 - n)) * sign
    incl2 = diff2 >= 0
    strict2 = diff2 > 0
    eye_right = jnp.where(jnp.logical_and(jnp.logical_not(left), diff2 == 0), 1.0, 0.0)

    lw = lw_ref[0]
    lhi, llo = _split(lw)
    cum = _dot(tri, lhi) + _dot(tri, llo)
    tot = jnp.where(dr == 0, cum[n - 1:n, :], cum[0:1, :])
    g_inv = jnp.exp(-cum)
    g_rest = jnp.exp(tot - cum)
    g_tot = jnp.exp(tot)
    kd = kd_ref[0]
    bd = bd_ref[0]
    v = v_ref[...]
    nr = _split(jnp.concatenate([nk_ref[...] * jnp.exp(cum - lw), r_ref[...] * jnp.exp(cum)], axis=0))
    bk = _split(jnp.concatenate([bd * g_inv, kd * g_inv], axis=0))
    bkh = _split(jnp.concatenate([bd * g_rest, kd * g_rest], axis=0))
    vs = _split(v)

    ps = SCAN_PASSES
    aa = [_mm(_cols(nr, c), _cols(bk, c), _NT, ps["aa"]) for c in hs]
    top = [jnp.where(strict2, a[:n], 0.0) for a in aa]
    bot = [_split(jnp.where(incl2, a[n:], 0.0)) for a in aa]
    tops = [_split(t) for t in top]
    x = [_mm(_cols(tp, hi_half), _cols(vs, c), _NN, ps["x"]) for tp, c in zip(tops, hs)]
    shift = INV_BASE.bit_length() - 1
    same_base = jnp.right_shift(row2, shift) == jnp.right_shift(jnp.where(left, lane2, lane2 - n), shift)
    slab = [jnp.where(left, jnp.where(same_base, t, 0.0), eye_right) for t in top]
    m = 1
    while m < INV_BASE:
        sp = [_split(sb) for sb in slab]
        slab = [_mm(_cols(p, lo_half), p, _NN, ps["neu"]) + jnp.where(left, 0.0, sb) for p, sb in zip(sp, slab)]
        m *= 2
    tinv = [sb[:, hi_half] for sb in slab]
    a_sq = [t[:, lo_half] for t in top]
    b = INV_BASE
    while b < n:
        sb_, s2b = b.bit_length() - 1, b.bit_length()
        off = jnp.logical_and(jnp.right_shift(row, s2b) == jnp.right_shift(col, s2b),
                              jnp.right_shift(row, sb_) != jnp.right_shift(col, sb_))
        ts = [_split(t) for t in tinv]
        z = [_mm(_split(jnp.where(off, a, 0.0)), t, _NN, ps["neu"]) for a, t in zip(a_sq, ts)]
        tinv = [t + _mm(tp, _split(zz), _NN, ps["neu"]) for t, tp, zz in zip(tinv, ts, z)]
        b *= 2
    rhs = [_cat([_rows(_cols(nr, c), lo_half), _split(xh)], axis=1) for c, xh in zip(hs, x)]
    wu = [_mm(_split(t), q, _NN, ps["wu"]) for t, q in zip(tinv, rhs)]
    st = [s_scr[h] for h in heads]
    gs = [_mm(_cat([_cols(_split(w), lo_half), _rows(_cols(nr, c), hi_half)], axis=0), _split(sh), _NT, ps["gs"])
          for w, c, sh in zip(wu, hs, st)]
    uv = [_split(jnp.concatenate([g[:n] + w[:, hi_half], v[:, c]], axis=0)) for g, w, c in zip(gs, wu, hs)]
    for h in heads:
        y_ref[0, :, hs[h]] = gs[h][n:] + _mm(bot[h], uv[h], _NN, ps["y"])
    for h in heads:
        s_scr[h] = st[h] * g_tot[:, hs[h]] + _mm(uv[h], _cols(bkh, hs[h]), _TN, ps["up"])

    hT_ref[0, 0] = s_scr[...]


def _scan_tables(seqs):
    blk = [[], []]
    first, seq = [], []
    sid = 0
    for tok0, nb, t in seqs:
        nc = t // SCAN_L
        for b in range(nb):
            base = (tok0 + b * t) // SCAN_L
            for c in range(nc):
                blk[0].append(base + c)
                blk[1].append(base + nc - 1 - c)
                first.append(1 if c == 0 else 0)
                seq.append(sid)
            sid += 1
    return (jnp.asarray(np.array(blk, np.int32).reshape(-1)), jnp.asarray(np.array(first, np.int32)),
            jnp.asarray(np.array(seq, np.int32)), len(first))


def _scan(lw, kd, bd, r, v, nk, h0, seqs):
    n_tok = r.shape[0]
    n_seq = h0.shape[0]
    blk, first, seq, n_steps = _scan_tables(seqs)
    hp = SCAN_W // HEAD
    n_pair = N_HEAD // hp

    def dir_map(d, p, s, blk, first, seq):
        return (d, blk[d * n_steps + s], p)

    def tok_map(d, p, s, blk, first, seq):
        return (blk[d * n_steps + s], p)

    def st_map(d, p, s, blk, first, seq):
        return (seq[s], d, p, 0, 0)

    dir_spec = pl.BlockSpec((1, SCAN_L, SCAN_W), dir_map)
    tok_spec = pl.BlockSpec((SCAN_L, SCAN_W), tok_map)
    st_spec = pl.BlockSpec((1, 1, hp, HEAD, HEAD), st_map)
    return pl.pallas_call(
        _scan_kernel,
        grid_spec=pltpu.PrefetchScalarGridSpec(
            num_scalar_prefetch=3,
            grid=(2, n_pair, n_steps),
            in_specs=[dir_spec, dir_spec, dir_spec, tok_spec, tok_spec, tok_spec, st_spec],
            out_specs=[dir_spec, st_spec],
            scratch_shapes=[pltpu.VMEM((hp, HEAD, HEAD), F32)]),
        out_shape=[jax.ShapeDtypeStruct((2, n_tok, D_MODEL), F32),
                   jax.ShapeDtypeStruct((n_seq, 2, N_HEAD, HEAD, HEAD), F32)],
        compiler_params=_cparams(("parallel", "parallel", "arbitrary")),
        name="rwkv_scan",
    )(blk, first, seq, lw, kd, bd, r, v, nk, h0)


def _post_kernel(y0_ref, y1_ref, bv_ref, g_ref, lg_ref, lb_ref, e_ref, et_ref, w_ref, o_ref):
    y = y0_ref[0] + y1_ref[0]
    mu = _seg_sum(y, e_ref, et_ref) * (1.0 / HEAD)
    yc = y - mu
    var = _seg_sum(yc * yc, e_ref, et_ref) * (1.0 / HEAD)
    yn = yc * lax.rsqrt(var + GN_EPS) * lg_ref[...] + lb_ref[...]
    o_ref[...] = _dot(((yn + bv_ref[...]) * g_ref[...]).astype(BF16), w_ref[...])


def _post(y, bv, g, lp):
    n_tok = bv.shape[0]
    tm = TM_TOK
    tok_spec = pl.BlockSpec((tm, D_MODEL), lambda i: (i, 0))

    def const(shape):
        return pl.BlockSpec(shape, lambda i: (0,) * len(shape))

    return pl.pallas_call(
        _post_kernel,
        grid=(n_tok // tm,),
        in_specs=[pl.BlockSpec((1, tm, D_MODEL), lambda i: (0, i, 0)),
                  pl.BlockSpec((1, tm, D_MODEL), lambda i: (1, i, 0)),
                  tok_spec, tok_spec, const((1, D_MODEL)), const((1, D_MODEL)),
                  const((D_MODEL, LANE)), const((LANE, D_MODEL)), const((D_MODEL, D_MODEL))],
        out_specs=tok_spec,
        out_shape=jax.ShapeDtypeStruct((n_tok, D_MODEL), F32),
        compiler_params=_cparams(("parallel",)),
        name="rwkv_post",
    )(y, y, bv, g, lp["lnx_g"], lp["lnx_b"], lp["e"], lp["et"], lp["w_branch_a"])


def _cmlp_kernel(zu_ref, zv_ref, lg_ref, ws_ref, bs_ref, w_ref, o_ref, y_scr):
    v = jax.nn.gelu(zv_ref[...])
    mu = jnp.mean(v, axis=-1, keepdims=True)
    vc = v - mu
    var = jnp.mean(vc * vc, axis=-1, keepdims=True)
    vn = (vc * lax.rsqrt(var + EPS) * lg_ref[...]).astype(BF16)
    u = jax.nn.gelu(zu_ref[...])
    for c in range(zu_ref.shape[0] // CHUNK):
        rows = slice(c * CHUNK, (c + 1) * CHUNK)
        for h in range(H_B):
            cols = slice(h * HEAD_B, (h + 1) * HEAD_B)
            s = _dot(ws_ref[h], vn[rows, cols]) + bs_ref[:, cols]
            y_scr[rows, cols] = (u[rows, cols] * s).astype(BF16)
    o_ref[...] = _dot(y_scr[...], w_ref[...])


def _cmlp(z_rest, lp):
    n_tok = z_rest.shape[0]
    tm = TM_TOK

    def const(shape):
        return pl.BlockSpec(shape, lambda i: (0,) * len(shape))

    return pl.pallas_call(
        _cmlp_kernel,
        grid=(n_tok // tm,),
        in_specs=[pl.BlockSpec((tm, D_MODEL), lambda i: (i, 0)),
                  pl.BlockSpec((tm, D_MODEL), lambda i: (i, 1)),
                  const((1, D_MODEL)), const((H_B, CHUNK, CHUNK)), const((CHUNK, D_MODEL)),
                  const((D_MODEL, D_MODEL))],
        out_specs=pl.BlockSpec((tm, D_MODEL), lambda i: (i, 0)),
        out_shape=jax.ShapeDtypeStruct((n_tok, D_MODEL), F32),
        scratch_shapes=[pltpu.VMEM((tm, D_MODEL), BF16)],
        compiler_params=_cparams(("parallel",)),
        name="chunk_mlp",
    )(z_rest, z_rest, lp["ln_v_g"], lp["w_s"], lp["b_s"], lp["w_branch_b"])


def _ffn_kernel(x_ref, ya_ref, yb_ref, ga_ref, gb_ref, mod_ref, g2_ref, wo_ref, w1_ref, w2_ref, fg_ref,
                o_ref, *maybe_final, final):
    m = mod_ref[0]
    mixed = jax.nn.sigmoid(ga_ref[...]) * ya_ref[...] + jax.nn.sigmoid(gb_ref[...]) * yb_ref[...]
    x = x_ref[...] + m[2:3] * _dot(mixed.astype(BF16), wo_ref[...])
    h2 = _rms_mod(x, g2_ref[...], m[3:4], m[4:5]).astype(BF16)
    acc = jnp.zeros(x.shape, F32)
    ff_chunk = D_MODEL
    for c in range(D_FF // ff_chunk):
        cols = slice(c * ff_chunk, (c + 1) * ff_chunk)
        hid = jnp.square(jnp.maximum(_dot(h2, w1_ref[:, cols]), 0.0)).astype(BF16)
        acc = acc + _dot(hid, w2_ref[cols, :])
    x = x + m[5:6] * acc
    o_ref[...] = x
    if final:
        yf = x * lax.rsqrt(jnp.mean(x * x, axis=-1, keepdims=True) + EPS) * fg_ref[...]
        maybe_final[0][...] = yf


def _ffn(x, ya, yb, z_rest, mod_l, lp, final_g, mod_row, final):
    n_tok = x.shape[0]
    tm = TM_TOK
    tok_spec = pl.BlockSpec((tm, D_MODEL), lambda i: (i, 0))

    def const(shape):
        return pl.BlockSpec(shape, lambda i: (0,) * len(shape), pipeline_mode=pl.Buffered(1))

    tok_shape = jax.ShapeDtypeStruct((n_tok, D_MODEL), F32)
    return pl.pallas_call(
        functools.partial(_ffn_kernel, final=final),
        grid=(n_tok // tm,),
        in_specs=[tok_spec, tok_spec, tok_spec,
                  pl.BlockSpec((tm, D_MODEL), lambda i: (i, 2)),
                  pl.BlockSpec((tm, D_MODEL), lambda i: (i, 3)),
                  pl.BlockSpec((1, N_MOD, D_MODEL), lambda i: (mod_row(i), 0, 0)),
                  const((1, D_MODEL)), const((D_MODEL, D_MODEL)), const((D_MODEL, D_FF)),
                  const((D_FF, D_MODEL)), const((1, D_MODEL))],
        out_specs=[tok_spec, tok_spec] if final else [tok_spec],
        out_shape=[tok_shape, tok_shape] if final else [tok_shape],
        compiler_params=_cparams(("parallel",)),
        name="mix_ffn",
    )(x, ya, yb, z_rest, z_rest, mod_l, lp["norm2_g"], lp["w_out"], lp["w1"], lp["w2"], final_g)


def _block_diag2(m):
    z = jnp.zeros_like(m[0])
    return jnp.concatenate([jnp.concatenate([m[0], z], axis=1), jnp.concatenate([z, m[1]], axis=1)], axis=0)


def _layer_params(l, w_in, mu_shift, w0, w_up, a0, a_up, g_up, k_k, k_a, r_k, lnx_g, lnx_b, w_branch_a,
                  ln_v_g, w_s, b_s, w_branch_b, w_out, w1, w2, norm1_g, norm2_g):
    head_of = np.arange(D_MODEL) // HEAD
    e = (head_of[:, None] == np.arange(LANE)[None, :]).astype(np.float32)
    row = lambda a: a.reshape(1, -1)
    return dict(
        w_in_rwkv=w_in[l][:, :C_RWKV].astype(BF16), w_in_rest=w_in[l][:, C_RWKV:].astype(BF16),
        mu=row(mu_shift[l]), wup=_block_diag2(w_up[l]).astype(BF16), aup=_block_diag2(a_up[l]).astype(BF16),
        gup=g_up[l].astype(BF16), w0=row(w0[l]), a0=row(a0[l]), k_k=row(k_k[l]), k_a=row(k_a[l]),
        r_k=row(r_k[l]), lnx_g=row(lnx_g[l]), lnx_b=row(lnx_b[l]), w_branch_a=w_branch_a[l].astype(BF16),
        ln_v_g=row(ln_v_g[l]), w_s=w_s[l].astype(BF16), b_s=jnp.repeat(b_s[l].T, HEAD_B, axis=1),
        w_branch_b=w_branch_b[l].astype(BF16), w_out=w_out[l].astype(BF16), w1=w1[l].astype(BF16),
        w2=w2[l].astype(BF16), norm1_g=row(norm1_g[l]), norm2_g=row(norm2_g[l]),
        e=jnp.asarray(e, BF16), et=jnp.asarray(e.T, BF16))


def kernel(x_prompt, x_sample, state_rwkv, c, c_ctx, w_ada, b_ada, norm1_g, norm2_g, w_in, mu_shift, w0, w_up,
           a0, a_up, g_up, k_k, k_a, r_k, lnx_g, lnx_b, w_branch_a, ln_v_g, w_s, b_s, w_branch_b, w_out, w1, w2,
           final_g):
    b_ctx, t_ctx, _ = x_prompt.shape
    b_lat, t_lat, _ = x_sample.shape
    depth = w_in.shape[0]
    n_ctx = b_ctx * t_ctx
    n_lat = b_lat * t_lat
    assert t_ctx == TM_TOK and t_lat % TM_TOK == 0 and t_lat % GRID_W == 0

    x = jnp.concatenate([x_prompt.reshape(n_ctx, D_MODEL), x_sample.reshape(n_lat, D_MODEL)], axis=0)
    cond = jnp.concatenate([c_ctx[None, :], c], axis=0)
    mod = _modulation(cond, w_ada, b_ada).reshape(depth, 1 + b_lat, N_MOD, D_MODEL)
    seqs = [(0, b_ctx, t_ctx), (n_ctx, b_lat, t_lat)]
    s_zero = jnp.zeros((b_ctx, 2, N_HEAD, HEAD, HEAD), F32)
    final_row = final_g.reshape(1, D_MODEL)

    states = []
    y_final = None
    for l in range(depth):
        lp = _layer_params(l, w_in, mu_shift, w0, w_up, a0, a_up, g_up, k_k, k_a, r_k, lnx_g, lnx_b,
                           w_branch_a, ln_v_g, w_s, b_s, w_branch_b, w_out, w1, w2, norm1_g, norm2_g)
        row512 = _mod_row_map(n_ctx, t_lat, 512)
        row_tm = _mod_row_map(n_ctx, t_lat, TM_TOK)
        z_rwkv = _in_proj(x, mod[l], lp["norm1_g"], lp["w_in_rwkv"], 1152, row512)
        z_rest = _in_proj(x, mod[l], lp["norm1_g"], lp["w_in_rest"], 1024, row512)
        r, v, nk, g, bv, lw, kd, bd = _prep(z_rwkv, lp, n_ctx, t_lat)
        h0 = jnp.concatenate([s_zero, state_rwkv[:, l]], axis=0)
        y, h_fin = _scan(lw, kd, bd, r, v, nk, h0, seqs)
        states.append(h_fin[:b_ctx])
        ya = _post(y, bv, g, lp)
        yb = _cmlp(z_rest, lp)
        outs = _ffn(x, ya, yb, z_rest, mod[l], lp, final_row, row_tm, final=(l == depth - 1))
        x = outs[0]
        if l == depth - 1:
            y_final = outs[1]

    y_prompt = y_final[:n_ctx].reshape(b_ctx, t_ctx, D_MODEL)
    y_sample = y_final[n_ctx:].reshape(b_lat, t_lat, D_MODEL)
    return (y_prompt, y_sample, jnp.stack(states, axis=1))
```

```python
import functools
import math

import numpy as np
import jax
import jax.numpy as jnp
from jax import lax
from jax.experimental import pallas as pl
from jax.experimental.pallas import tpu as pltpu

F32 = jnp.float32
BF16 = jnp.bfloat16

D_MODEL = 1024
HEAD = 64
N_HEAD = D_MODEL // HEAD
LORA = 64
G_LORA = 128
C_RWKV = 3 * D_MODEL + 4 * LORA + G_LORA
D_REST = 4 * D_MODEL
D_FF = 4 * D_MODEL
GRID_W = 64
CHUNK = 128
H_B = 8
HEAD_B = D_MODEL // H_B
N_MOD = 6
EPS = 1e-6
GN_EPS = 64e-5
DECAY_SCALE = math.exp(-0.5)

SCAN_L = 64
LANE = 128
SCAN_PASSES = dict(aa=1, x=1, neu=1, wu=1, gs=1, y=1, up=1)
INV_BASE = 8
TM_TOK = 256
VMEM_LIMIT = 56 * 1024 * 1024


def _cparams(sem):
    return pltpu.CompilerParams(dimension_semantics=sem, vmem_limit_bytes=VMEM_LIMIT)


def _split(x):
    hi = x.astype(BF16)
    lo = (x - hi.astype(F32)).astype(BF16)
    return hi, lo


def _dot(a, b, dims=(((1,), (0,)), ((), ()))):
    return lax.dot_general(a, b, dims, preferred_element_type=F32)


_NN = (((1,), (0,)), ((), ()))
_NT = (((1,), (1,)), ((), ()))
_TN = (((0,), (0,)), ((), ()))


def _dot3(a, b, dims=_NN):
    ah, al = _split(a)
    bh, bl = _split(b)
    return _dot(ah, bh, dims) + (_dot(ah, bl, dims) + _dot(al, bh, dims))


def _seg_sum(x, e_ref, et_ref):
    hi, lo = _split(x)
    s = _dot(hi, e_ref[...]) + _dot(lo, e_ref[...])
    shi, slo = _split(s)
    return _dot(shi, et_ref[...]) + _dot(slo, et_ref[...])


def _mod_kernel(c_ref, w_ref, b_ref, o_ref):
    cond = c_ref[...]
    o_ref[0] = _dot(jax.nn.silu(cond).astype(BF16), w_ref[0].astype(BF16)) + b_ref[0]


def _modulation(cond, w_ada, b_ada):
    depth = w_ada.shape[0]
    n = cond.shape[0]
    tn = 1536
    return pl.pallas_call(
        _mod_kernel,
        grid=(depth, (N_MOD * D_MODEL) // tn),
        in_specs=[pl.BlockSpec((n, D_MODEL), lambda l, j: (0, 0)),
                  pl.BlockSpec((1, D_MODEL, tn), lambda l, j: (l, 0, j)),
                  pl.BlockSpec((1, 1, tn), lambda l, j: (l, 0, j))],
        out_specs=pl.BlockSpec((1, n, tn), lambda l, j: (l, 0, j)),
        out_shape=jax.ShapeDtypeStruct((depth, n, N_MOD * D_MODEL), F32),
        compiler_params=_cparams(("parallel", "parallel")),
        name="modulation",
    )(cond, w_ada, b_ada.reshape(depth, 1, N_MOD * D_MODEL))


def _mod_row_map(n_ctx_tok, t_lat, tm):
    def row(i):
        tok = i * tm
        return jnp.where(tok < n_ctx_tok, 0, 1 + (tok - n_ctx_tok) // t_lat)
    return row


def _rms_mod(x, g, shift, scale):
    y = x * lax.rsqrt(jnp.mean(x * x, axis=-1, keepdims=True) + EPS) * g
    return y * (1.0 + scale) + shift


def _in_proj_kernel(x_ref, mod_ref, g_ref, w_ref, o_ref, h_scr):
    @pl.when(pl.program_id(1) == 0)
    def _():
        m = mod_ref[0]
        h_scr[...] = _rms_mod(x_ref[...], g_ref[...], m[0:1], m[1:2]).astype(BF16)

    o_ref[...] = _dot(h_scr[...], w_ref[...])


def _in_proj(x, mod_l, g, w, tn, mod_row, tm=512):
    n_tok = x.shape[0]
    n_out = w.shape[1]
    return pl.pallas_call(
        _in_proj_kernel,
        grid=(n_tok // tm, n_out // tn),
        in_specs=[pl.BlockSpec((tm, D_MODEL), lambda i, j: (i, 0)),
                  pl.BlockSpec((1, N_MOD, D_MODEL), lambda i, j: (mod_row(i), 0, 0)),
                  pl.BlockSpec((1, D_MODEL), lambda i, j: (0, 0)),
                  pl.BlockSpec((D_MODEL, tn), lambda i, j: (0, j))],
        out_specs=pl.BlockSpec((tm, tn), lambda i, j: (i, j)),
        out_shape=jax.ShapeDtypeStruct((n_tok, n_out), F32),
        scratch_shapes=[pltpu.VMEM((tm, D_MODEL), BF16)],
        compiler_params=_cparams(("parallel", "arbitrary")),
        name="in_proj",
    )(x, mod_l, g, w)


def _shift_rows(z, k, fill_first):
    return jnp.concatenate([fill_first, z[: z.shape[0] - k]], axis=0)


def _prep_kernel(z_ref, zp_ref, zn_ref, mu_ref, wup_ref, aup_ref, gup_ref, w0_ref, a0_ref, kk_ref, ka_ref,
                 rk_ref, e_ref, et_ref,
                 r_ref, v_ref, nk_ref, g_ref, bv_ref, lw_ref, kd_ref, bd_ref, zs_scr,
                 *, n_ctx_tiles, lat_tiles_per_seq):
    i = pl.program_id(0)
    tm = z_ref.shape[0]
    z = z_ref[...]
    row = lax.broadcasted_iota(jnp.int32, z.shape, 0)
    lane = lax.broadcasted_iota(jnp.int32, z.shape, 1)
    prev1 = jnp.where(row == 0, 0.0, pltpu.roll(z, 1, 0))
    next1 = jnp.where(row == tm - 1, 0.0, pltpu.roll(z, tm - 1, 0))

    @pl.when(i < n_ctx_tiles)
    def _():
        zs_scr[...] = jnp.where(lane % 2 == 0, prev1, next1)

    @pl.when(i >= n_ctx_tiles)
    def _():
        j = (i - n_ctx_tiles) % lat_tiles_per_seq
        col = row % GRID_W
        left = jnp.where(col == 0, 0.0, prev1)
        right = jnp.where(col == GRID_W - 1, 0.0, next1)
        up_halo = jnp.where(j == 0, 0.0, zp_ref[...])
        dn_halo = jnp.where(j == lat_tiles_per_seq - 1, 0.0, zn_ref[...])
        up = jnp.concatenate([up_halo, z[: tm - GRID_W]], axis=0)
        down = jnp.concatenate([z[GRID_W:], dn_halo], axis=0)
        m = lane % 4
        zs_scr[...] = jnp.where(m == 0, left, jnp.where(m == 1, right, jnp.where(m == 2, up, down)))

    zs = z + mu_ref[...] * (zs_scr[...] - z)
    d = D_MODEL
    r = zs[:, 0:d]
    k = zs[:, d:2 * d]
    v = zs[:, 2 * d:3 * d]
    wd = zs[:, 3 * d:3 * d + 2 * LORA]
    ad = zs[:, 3 * d + 2 * LORA:3 * d + 4 * LORA]
    gd = zs[:, 3 * d + 4 * LORA:]
    r_ref[...] = r
    v_ref[...] = v
    g_ref[...] = _dot(jax.nn.sigmoid(gd).astype(BF16), gup_ref[...])
    w_logit = w0_ref[...] + _dot(jnp.tanh(wd).astype(BF16), wup_ref[...])
    a_all = jax.nn.sigmoid(a0_ref[...] + _dot(ad.astype(BF16), aup_ref[...]))
    kk = k * kk_ref[...]
    kk = kk * lax.rsqrt(_seg_sum(kk * kk, e_ref, et_ref) + 1e-12)
    nk_ref[...] = -kk
    ka = ka_ref[...]
    kd_sum = None
    for dr in range(2):
        a = a_all[:, dr * d:(dr + 1) * d]
        lw_ref[dr] = -DECAY_SCALE * jax.nn.sigmoid(w_logit[:, dr * d:(dr + 1) * d])
        kd = k * (1.0 + (a - 1.0) * ka)
        kd_ref[dr] = kd
        bd_ref[dr] = a * kk
        kd_sum = kd if kd_sum is None else kd_sum + kd
    bv_ref[...] = _seg_sum(kd_sum * rk_ref[...] * r, e_ref, et_ref) * v


def _prep(z_rwkv, lp, n_ctx_tok, t_lat):
    n_tok = z_rwkv.shape[0]
    tm = TM_TOK
    hb = tm // GRID_W
    n_hblk = n_tok // GRID_W
    tok_spec = pl.BlockSpec((tm, D_MODEL), lambda i: (i, 0))
    dir_spec = pl.BlockSpec((2, tm, D_MODEL), lambda i: (0, i, 0))

    def const(shape):
        return pl.BlockSpec(shape, lambda i: (0,) * len(shape))

    kern = functools.partial(_prep_kernel, n_ctx_tiles=n_ctx_tok // tm, lat_tiles_per_seq=t_lat // tm)
    tok_shape = jax.ShapeDtypeStruct((n_tok, D_MODEL), F32)
    dir_shape = jax.ShapeDtypeStruct((2, n_tok, D_MODEL), F32)
    return pl.pallas_call(
        kern,
        grid=(n_tok // tm,),
        in_specs=[pl.BlockSpec((tm, C_RWKV), lambda i: (i, 0)),
                  pl.BlockSpec((GRID_W, C_RWKV), lambda i: (jnp.maximum(i * hb - 1, 0), 0)),
                  pl.BlockSpec((GRID_W, C_RWKV), lambda i: (jnp.minimum((i + 1) * hb, n_hblk - 1), 0)),
                  const((1, C_RWKV)), const((2 * LORA, 2 * D_MODEL)), const((2 * LORA, 2 * D_MODEL)),
                  const((G_LORA, D_MODEL)), const((1, 2 * D_MODEL)), const((1, 2 * D_MODEL)),
                  const((1, D_MODEL)), const((1, D_MODEL)), const((1, D_MODEL)),
                  const((D_MODEL, LANE)), const((LANE, D_MODEL))],
        out_specs=[tok_spec, tok_spec, tok_spec, tok_spec, tok_spec, dir_spec, dir_spec, dir_spec],
        out_shape=[tok_shape] * 5 + [dir_shape] * 3,
        scratch_shapes=[pltpu.VMEM((tm, C_RWKV), F32)],
        compiler_params=_cparams(("parallel",)),
        name="rwkv_prep",
    )(z_rwkv, z_rwkv, z_rwkv, lp["mu"], lp["wup"], lp["aup"], lp["gup"], lp["w0"], lp["a0"], lp["k_k"],
      lp["k_a"], lp["r_k"], lp["e"], lp["et"])


def _mm(a, b, dims=_NN, passes=3):
    out = _dot(a[0], b[0], dims)
    if passes == 3:
        out = out + (_dot(a[0], b[1], dims) + _dot(a[1], b[0], dims))
    return out


def _cols(p, sl):
    return (p[0][:, sl], p[1][:, sl])


def _rows(p, sl):
    return (p[0][sl], p[1][sl])


def _cat(ps, axis):
    return (jnp.concatenate([p[0] for p in ps], axis=axis), jnp.concatenate([p[1] for p in ps], axis=axis))


def _scan_kernel(blk_ref, first_ref, seq_ref, *refs):
    del blk_ref, seq_ref
    in_refs = (refs[0:6], refs[6:12])
    h0_ref, y_refs, hT_ref, s_scr = refs[12], refs[13:15], refs[15], refs[16]
    s = pl.program_id(0)
    n = in_refs[0][0].shape[1]
    heads = range(in_refs[0][0].shape[2] // HEAD)
    hs = [slice(h * HEAD, (h + 1) * HEAD) for h in heads]
    lo_half = slice(0, n)
    hi_half = slice(n, 2 * n)

    @pl.when(first_ref[s] == 1)
    def _():
        s_scr[...] = h0_ref[0]

    row = lax.broadcasted_iota(jnp.int32, (n, n), 0)
    col = lax.broadcasted_iota(jnp.int32, (n, n), 1)
    row2 = lax.broadcasted_iota(jnp.int32, (n, 2 * n), 0)
    lane2 = lax.broadcasted_iota(jnp.int32, (n, 2 * n), 1)
    left = lane2 < n
    col2 = jnp.where(left, lane2, lane2 - n)
    eye_right = jnp.where(jnp.logical_and(jnp.logical_not(left), row2 == col2), 1.0, 0.0)

    incl2, strict2, nr, bk, bkh, vs, v, g_tot = [], [], [], [], [], [], [], []
    for dr in range(2):
        lw_ref, kd_ref, bd_ref, r_ref, v_ref, nk_ref = in_refs[dr]
        sign = 1 - 2 * dr
        tri = ((row - col) * sign >= 0).astype(BF16)
        incl2.append((row2 - col2) * sign >= 0)
        strict2.append((row2 - col2) * sign > 0)
        lw = lw_ref[0]
        lhi, llo = _split(lw)
        cum = _dot(tri, lhi) + _dot(tri, llo)
        tot = cum[n - 1:n, :] if dr == 0 else cum[0:1, :]
        g_inv = jnp.exp(-cum)
        g_rest = jnp.exp(tot - cum)
        g_tot.append(jnp.exp(tot))
        kd = kd_ref[0]
        bd = bd_ref[0]
        v.append(v_ref[...])
        nr.append(_split(jnp.concatenate([nk_ref[...] * jnp.exp(cum - lw), r_ref[...] * jnp.exp(cum)], axis=0)))
        bk.append(_split(jnp.concatenate([bd * g_inv, kd * g_inv], axis=0)))
        bkh.append(_split(jnp.concatenate([bd * g_rest, kd * g_rest], axis=0)))
        vs.append(_split(v[dr]))

    units = [(dr, h) for h in heads for dr in range(2)]
    ps = SCAN_PASSES
    aa = [_mm(_cols(nr[d], hs[h]), _cols(bk[d], hs[h]), _NT, ps["aa"]) for d, h in units]
    top = [jnp.where(strict2[d], a[:n], 0.0) for a, (d, h) in zip(aa, units)]
    bot = [_split(jnp.where(incl2[d], a[n:], 0.0)) for a, (d, h) in zip(aa, units)]
    tops = [_split(t) for t in top]
    x = [_mm(_cols(tp, hi_half), _cols(vs[d], hs[h]), _NN, ps["x"]) for tp, (d, h) in zip(tops, units)]
    shift = INV_BASE.bit_length() - 1
    same_base = jnp.right_shift(row2, shift) == jnp.right_shift(col2, shift)
    slab = [jnp.where(left, jnp.where(same_base, t, 0.0), eye_right) for t in top]
    m = 1
    while m < INV_BASE:
        sp = [_split(sb) for sb in slab]
        slab = [_mm(_cols(p, lo_half), p, _NN, ps["neu"]) + jnp.where(left, 0.0, sb) for p, sb in zip(sp, slab)]
        m *= 2
    tinv = [sb[:, hi_half] for sb in slab]
    a_sq = [t[:, lo_half] for t in top]
    b = INV_BASE
    while b < n:
        sb_, s2b = b.bit_length() - 1, b.bit_length()
        off = jnp.logical_and(jnp.right_shift(row, s2b) == jnp.right_shift(col, s2b),
                              jnp.right_shift(row, sb_) != jnp.right_shift(col, sb_))
        ts = [_split(t) for t in tinv]
        z = [_mm(_split(jnp.where(off, a, 0.0)), t, _NN, ps["neu"]) for a, t in zip(a_sq, ts)]
        tinv = [t + _mm(tp, _split(zz), _NN, ps["neu"]) for t, tp, zz in zip(tinv, ts, z)]
        b *= 2
    rhs = [_cat([_rows(_cols(nr[d], hs[h]), lo_half), _split(xh)], axis=1) for (d, h), xh in zip(units, x)]
    wu = [_mm(_split(t), q, _NN, ps["wu"]) for t, q in zip(tinv, rhs)]
    st = [s_scr[d, h] for d, h in units]
    gs = [_mm(_cat([_cols(_split(w), lo_half), _rows(_cols(nr[d], hs[h]), hi_half)], axis=0), _split(sh), _NT,
              ps["gs"]) for w, (d, h), sh in zip(wu, units, st)]
    uv = [_split(jnp.concatenate([g[:n] + w[:, hi_half], v[d][:, hs[h]]], axis=0))
          for g, w, (d, h) in zip(gs, wu, units)]
    for i, (d, h) in enumerate(units):
        y_refs[d][:, hs[h]] = gs[i][n:] + _mm(bot[i], uv[i], _NN, ps["y"])
    for i, (d, h) in enumerate(units):
        s_scr[d, h] = st[i] * g_tot[d][:, hs[h]] + _mm(uv[i], _cols(bkh[d], hs[h]), _TN, ps["up"])

    hT_ref[0] = s_scr[...]


def _scan_tables(seqs):
    blk = [[], []]
    first, seq = [], []
    sid = 0
    for tok0, nb, t in seqs:
        nc = t // SCAN_L
        for b in range(nb):
            base = (tok0 + b * t) // SCAN_L
            for c in range(nc):
                blk[0].append(base + c)
                blk[1].append(base + nc - 1 - c)
                first.append(1 if c == 0 else 0)
                seq.append(sid)
            sid += 1
    return (jnp.asarray(np.array(blk, np.int32).reshape(-1)), jnp.asarray(np.array(first, np.int32)),
            jnp.asarray(np.array(seq, np.int32)), len(first))


def _scan(lw, kd, bd, r, v, nk, h0, seqs):
    assert SCAN_L == HEAD
    n_tok = r.shape[0]
    n_seq = h0.shape[0]
    blk, first, seq, n_steps = _scan_tables(seqs)

    def dir_spec(d):
        return pl.BlockSpec((1, SCAN_L, D_MODEL), lambda s, blk, first, seq: (d, blk[d * n_steps + s], 0))

    def tok_spec(d):
        return pl.BlockSpec((SCAN_L, D_MODEL), lambda s, blk, first, seq: (blk[d * n_steps + s], 0))

    st_spec = pl.BlockSpec((1, 2, N_HEAD, HEAD, HEAD), lambda s, blk, first, seq: (seq[s], 0, 0, 0, 0))
    in_specs, args = [], []
    for d in range(2):
        in_specs += [dir_spec(d)] * 3 + [tok_spec(d)] * 3
        args += [lw, kd, bd, r, v, nk]
    tok_shape = jax.ShapeDtypeStruct((n_tok, D_MODEL), F32)
    return pl.pallas_call(
        _scan_kernel,
        grid_spec=pltpu.PrefetchScalarGridSpec(
            num_scalar_prefetch=3,
            grid=(n_steps,),
            in_specs=in_specs + [st_spec],
            out_specs=[tok_spec(0), tok_spec(1), st_spec],
            scratch_shapes=[pltpu.VMEM((2, N_HEAD, HEAD, HEAD), F32)]),
        out_shape=[tok_shape, tok_shape, jax.ShapeDtypeStruct((n_seq, 2, N_HEAD, HEAD, HEAD), F32)],
        compiler_params=_cparams(("arbitrary",)),
        name="rwkv_scan",
    )(blk, first, seq, *args, h0)


def _post_kernel(y0_ref, y1_ref, bv_ref, g_ref, lg_ref, lb_ref, e_ref, et_ref, w_ref, o_ref):
    y = y0_ref[...] + y1_ref[...]
    mu = _seg_sum(y, e_ref, et_ref) * (1.0 / HEAD)
    yc = y - mu
    var = _seg_sum(yc * yc, e_ref, et_ref) * (1.0 / HEAD)
    yn = yc * lax.rsqrt(var + GN_EPS) * lg_ref[...] + lb_ref[...]
    o_ref[...] = _dot(((yn + bv_ref[...]) * g_ref[...]).astype(BF16), w_ref[...])


def _post(y0, y1, bv, g, lp):
    n_tok = bv.shape[0]
    tm = TM_TOK
    tok_spec = pl.BlockSpec((tm, D_MODEL), lambda i: (i, 0))

    def const(shape):
        return pl.BlockSpec(shape, lambda i: (0,) * len(shape))

    return pl.pallas_call(
        _post_kernel,
        grid=(n_tok // tm,),
        in_specs=[tok_spec, tok_spec, tok_spec, tok_spec, const((1, D_MODEL)), const((1, D_MODEL)),
                  const((D_MODEL, LANE)), const((LANE, D_MODEL)), const((D_MODEL, D_MODEL))],
        out_specs=tok_spec,
        out_shape=jax.ShapeDtypeStruct((n_tok, D_MODEL), F32),
        compiler_params=_cparams(("parallel",)),
        name="rwkv_post",
    )(y0, y1, bv, g, lp["lnx_g"], lp["lnx_b"], lp["e"], lp["et"], lp["w_branch_a"])


def _cmlp_kernel(zu_ref, zv_ref, lg_ref, ws_ref, bs_ref, w_ref, o_ref, y_scr):
    v = jax.nn.gelu(zv_ref[...])
    mu = jnp.mean(v, axis=-1, keepdims=True)
    vc = v - mu
    var = jnp.mean(vc * vc, axis=-1, keepdims=True)
    vn = (vc * lax.rsqrt(var + EPS) * lg_ref[...]).astype(BF16)
    u = jax.nn.gelu(zu_ref[...])
    for c in range(zu_ref.shape[0] // CHUNK):
        rows = slice(c * CHUNK, (c + 1) * CHUNK)
        for h in range(H_B):
            cols = slice(h * HEAD_B, (h + 1) * HEAD_B)
            s = _dot(ws_ref[h], vn[rows, cols]) + bs_ref[:, cols]
            y_scr[rows, cols] = (u[rows, cols] * s).astype(BF16)
    o_ref[...] = _dot(y_scr[...], w_ref[...])


def _cmlp(z_rest, lp):
    n_tok = z_rest.shape[0]
    tm = TM_TOK

    def const(shape):
        return pl.BlockSpec(shape, lambda i: (0,) * len(shape))

    return pl.pallas_call(
        _cmlp_kernel,
        grid=(n_tok // tm,),
        in_specs=[pl.BlockSpec((tm, D_MODEL), lambda i: (i, 0)),
                  pl.BlockSpec((tm, D_MODEL), lambda i: (i, 1)),
                  const((1, D_MODEL)), const((H_B, CHUNK, CHUNK)), const((CHUNK, D_MODEL)),
                  const((D_MODEL, D_MODEL))],
        out_specs=pl.BlockSpec((tm, D_MODEL), lambda i: (i, 0)),
        out_shape=jax.ShapeDtypeStruct((n_tok, D_MODEL), F32),
        scratch_shapes=[pltpu.VMEM((tm, D_MODEL), BF16)],
        compiler_params=_cparams(("parallel",)),
        name="chunk_mlp",
    )(z_rest, z_rest, lp["ln_v_g"], lp["w_s"], lp["b_s"], lp["w_branch_b"])


def _ffn_kernel(x_ref, ya_ref, yb_ref, ga_ref, gb_ref, mod_ref, g2_ref, wo_ref, w1_ref, w2_ref, fg_ref,
                o_ref, *maybe_final, final):
    m = mod_ref[0]
    mixed = jax.nn.sigmoid(ga_ref[...]) * ya_ref[...] + jax.nn.sigmoid(gb_ref[...]) * yb_ref[...]
    x = x_ref[...] + m[2:3] * _dot(mixed.astype(BF16), wo_ref[...])
    h2 = _rms_mod(x, g2_ref[...], m[3:4], m[4:5]).astype(BF16)
    acc = jnp.zeros(x.shape, F32)
    ff_chunk = D_MODEL
    for c in range(D_FF // ff_chunk):
        cols = slice(c * ff_chunk, (c + 1) * ff_chunk)
        hid = jnp.square(jnp.maximum(_dot(h2, w1_ref[:, cols]), 0.0)).astype(BF16)
        acc = acc + _dot(hid, w2_ref[cols, :])
    x = x + m[5:6] * acc
    o_ref[...] = x
    if final:
        yf = x * lax.rsqrt(jnp.mean(x * x, axis=-1, keepdims=True) + EPS) * fg_ref[...]
        maybe_final[0][...] = yf


def _ffn(x, ya, yb, z_rest, mod_l, lp, final_g, mod_row, final):
    n_tok = x.shape[0]
    tm = TM_TOK
    tok_spec = pl.BlockSpec((tm, D_MODEL), lambda i: (i, 0))

    def const(shape):
        return pl.BlockSpec(shape, lambda i: (0,) * len(shape), pipeline_mode=pl.Buffered(1))

    tok_shape = jax.ShapeDtypeStruct((n_tok, D_MODEL), F32)
    return pl.pallas_call(
        functools.partial(_ffn_kernel, final=final),
        grid=(n_tok // tm,),
        in_specs=[tok_spec, tok_spec, tok_spec,
                  pl.BlockSpec((tm, D_MODEL), lambda i: (i, 2)),
                  pl.BlockSpec((tm, D_MODEL), lambda i: (i, 3)),
                  pl.BlockSpec((1, N_MOD, D_MODEL), lambda i: (mod_row(i), 0, 0)),
                  const((1, D_MODEL)), const((D_MODEL, D_MODEL)), const((D_MODEL, D_FF)),
                  const((D_FF, D_MODEL)), const((1, D_MODEL))],
        out_specs=[tok_spec, tok_spec] if final else [tok_spec],
        out_shape=[tok_shape, tok_shape] if final else [tok_shape],
        compiler_params=_cparams(("parallel",)),
        name="mix_ffn",
    )(x, ya, yb, z_rest, z_rest, mod_l, lp["norm2_g"], lp["w_out"], lp["w1"], lp["w2"], final_g)


def _block_diag2(m):
    z = jnp.zeros_like(m[0])
    return jnp.concatenate([jnp.concatenate([m[0], z], axis=1), jnp.concatenate([z, m[1]], axis=1)], axis=0)


def _layer_params(l, w_in, mu_shift, w0, w_up, a0, a_up, g_up, k_k, k_a, r_k, lnx_g, lnx_b, w_branch_a,
                  ln_v_g, w_s, b_s, w_branch_b, w_out, w1, w2, norm1_g, norm2_g):
    head_of = np.arange(D_MODEL) // HEAD
    e = (head_of[:, None] == np.arange(LANE)[None, :]).astype(np.float32)
    row = lambda a: a.reshape(1, -1)
    return dict(
        w_in_rwkv=w_in[l][:, :C_RWKV].astype(BF16), w_in_rest=w_in[l][:, C_RWKV:].astype(BF16),
        mu=row(mu_shift[l]), wup=_block_diag2(w_up[l]).astype(BF16), aup=_block_diag2(a_up[l]).astype(BF16),
        gup=g_up[l].astype(BF16), w0=row(w0[l]), a0=row(a0[l]), k_k=row(k_k[l]), k_a=row(k_a[l]),
        r_k=row(r_k[l]), lnx_g=row(lnx_g[l]), lnx_b=row(lnx_b[l]), w_branch_a=w_branch_a[l].astype(BF16),
        ln_v_g=row(ln_v_g[l]), w_s=w_s[l].astype(BF16), b_s=jnp.repeat(b_s[l].T, HEAD_B, axis=1),
        w_branch_b=w_branch_b[l].astype(BF16), w_out=w_out[l].astype(BF16), w1=w1[l].astype(BF16),
        w2=w2[l].astype(BF16), norm1_g=row(norm1_g[l]), norm2_g=row(norm2_g[l]),
        e=jnp.asarray(e, BF16), et=jnp.asarray(e.T, BF16))


def kernel(x_prompt, x_sample, state_rwkv, c, c_ctx, w_ada, b_ada, norm1_g, norm2_g, w_in, mu_shift, w0, w_up,
           a0, a_up, g_up, k_k, k_a, r_k, lnx_g, lnx_b, w_branch_a, ln_v_g, w_s, b_s, w_branch_b, w_out, w1, w2,
           final_g):
    b_ctx, t_ctx, _ = x_prompt.shape
    b_lat, t_lat, _ = x_sample.shape
    depth = w_in.shape[0]
    n_ctx = b_ctx * t_ctx
    n_lat = b_lat * t_lat
    assert t_ctx == TM_TOK and t_lat % TM_TOK == 0 and t_lat % GRID_W == 0

    x = jnp.concatenate([x_prompt.reshape(n_ctx, D_MODEL), x_sample.reshape(n_lat, D_MODEL)], axis=0)
    cond = jnp.concatenate([c_ctx[None, :], c], axis=0)
    mod = _modulation(cond, w_ada, b_ada).reshape(depth, 1 + b_lat, N_MOD, D_MODEL)
    seqs = [(0, b_ctx, t_ctx), (n_ctx, b_lat, t_lat)]
    s_zero = jnp.zeros((b_ctx, 2, N_HEAD, HEAD, HEAD), F32)
    final_row = final_g.reshape(1, D_MODEL)

    states = []
    y_final = None
    for l in range(depth):
        lp = _layer_params(l, w_in, mu_shift, w0, w_up, a0, a_up, g_up, k_k, k_a, r_k, lnx_g, lnx_b,
                           w_branch_a, ln_v_g, w_s, b_s, w_branch_b, w_out, w1, w2, norm1_g, norm2_g)
        row512 = _mod_row_map(n_ctx, t_lat, 512)
        row_tm = _mod_row_map(n_ctx, t_lat, TM_TOK)
        z_rwkv = _in_proj(x, mod[l], lp["norm1_g"], lp["w_in_rwkv"], 1152, row512)
        z_rest = _in_proj(x, mod[l], lp["norm1_g"], lp["w_in_rest"], 1024, row512)
        r, v, nk, g, bv, lw, kd, bd = _prep(z_rwkv, lp, n_ctx, t_lat)
        h0 = jnp.concatenate([s_zero, state_rwkv[:, l]], axis=0)
        y0, y1, h_fin = _scan(lw, kd, bd, r, v, nk, h0, seqs)
        states.append(h_fin[:b_ctx])
        ya = _post(y0, y1, bv, g, lp)
        yb = _cmlp(z_rest, lp)
        outs = _ffn(x, ya, yb, z_rest, mod[l], lp, final_row, row_tm, final=(l == depth - 1))
        x = outs[0]
        if l == depth - 1:
            y_final = outs[1]

    y_prompt = y_final[:n_ctx].reshape(b_ctx, t_ctx, D_MODEL)
    y_sample = y_final[n_ctx:].reshape(b_lat, t_lat, D_MODEL)
    return (y_prompt, y_sample, jnp.stack(states, axis=1))
```

```python
import functools
import math

import numpy as np
import jax
import jax.numpy as jnp
from jax import lax
from jax.experimental import pallas as pl
from jax.experimental.pallas import tpu as pltpu

F32 = jnp.float32
BF16 = jnp.bfloat16

D_MODEL = 1024
HEAD = 64
N_HEAD = D_MODEL // HEAD
LORA = 64
G_LORA = 128
C_RWKV = 3 * D_MODEL + 4 * LORA + G_LORA
D_REST = 4 * D_MODEL
D_FF = 4 * D_MODEL
GRID_W = 64
CHUNK = 128
H_B = 8
HEAD_B = D_MODEL // H_B
N_MOD = 6
EPS = 1e-6
GN_EPS = 64e-5
DECAY_SCALE = math.exp(-0.5)

SCAN_L = 64
LANE = 128
SCAN_PASSES = dict(aa=1, x=1, neu=1, wu=1, gs=1, y=1, up=1)
INV_BASE = 8
TM_TOK = 256
TM_PROJ = 2048
VMEM_LIMIT = 56 * 1024 * 1024


def _cparams(sem):
    return pltpu.CompilerParams(dimension_semantics=sem, vmem_limit_bytes=VMEM_LIMIT)


def _split(x):
    hi = x.astype(BF16)
    lo = (x - hi.astype(F32)).astype(BF16)
    return hi, lo


def _dot(a, b, dims=(((1,), (0,)), ((), ()))):
    return lax.dot_general(a, b, dims, preferred_element_type=F32)


_NN = (((1,), (0,)), ((), ()))
_NT = (((1,), (1,)), ((), ()))
_TN = (((0,), (0,)), ((), ()))


def _dot3(a, b, dims=_NN):
    ah, al = _split(a)
    bh, bl = _split(b)
    return _dot(ah, bh, dims) + (_dot(ah, bl, dims) + _dot(al, bh, dims))


def _seg_sum(x, e_ref, et_ref):
    hi, lo = _split(x)
    s = _dot(hi, e_ref[...]) + _dot(lo, e_ref[...])
    shi, slo = _split(s)
    return _dot(shi, et_ref[...]) + _dot(slo, et_ref[...])


def _mod_kernel(c_ref, w_ref, b_ref, o_ref):
    cond = c_ref[...]
    o_ref[0] = _dot(jax.nn.silu(cond).astype(BF16), w_ref[0].astype(BF16)) + b_ref[0]


def _modulation(cond, w_ada, b_ada):
    depth = w_ada.shape[0]
    n = cond.shape[0]
    tn = 1536
    return pl.pallas_call(
        _mod_kernel,
        grid=(depth, (N_MOD * D_MODEL) // tn),
        in_specs=[pl.BlockSpec((n, D_MODEL), lambda l, j: (0, 0)),
                  pl.BlockSpec((1, D_MODEL, tn), lambda l, j: (l, 0, j)),
                  pl.BlockSpec((1, 1, tn), lambda l, j: (l, 0, j))],
        out_specs=pl.BlockSpec((1, n, tn), lambda l, j: (l, 0, j)),
        out_shape=jax.ShapeDtypeStruct((depth, n, N_MOD * D_MODEL), F32),
        compiler_params=_cparams(("parallel", "parallel")),
        name="modulation",
    )(cond, w_ada, b_ada.reshape(depth, 1, N_MOD * D_MODEL))


def _mod_row_map(n_ctx_tok, t_lat, tm):
    def row(i):
        tok = i * tm
        return jnp.where(tok < n_ctx_tok, 0, 1 + (tok - n_ctx_tok) // t_lat)
    return row


def _rms_mod(x, g, shift, scale):
    y = x * lax.rsqrt(jnp.mean(x * x, axis=-1, keepdims=True) + EPS) * g
    return y * (1.0 + scale) + shift


def _in_proj_kernel(x_ref, mod_ref, g_ref, w_ref, o_ref, h_scr):
    @pl.when(pl.program_id(1) == 0)
    def _():
        m = mod_ref[0]
        h_scr[...] = _rms_mod(x_ref[...], g_ref[...], m[0:1], m[1:2]).astype(BF16)

    o_ref[...] = _dot(h_scr[...], w_ref[...])


def _in_proj(x, mod_l, g, w, tn, mod_row, tm):
    n_tok = x.shape[0]
    n_out = w.shape[1]
    return pl.pallas_call(
        _in_proj_kernel,
        grid=(n_tok // tm, n_out // tn),
        in_specs=[pl.BlockSpec((tm, D_MODEL), lambda i, j: (i, 0)),
                  pl.BlockSpec((1, N_MOD, D_MODEL), lambda i, j: (mod_row(i), 0, 0)),
                  pl.BlockSpec((1, D_MODEL), lambda i, j: (0, 0)),
                  pl.BlockSpec((D_MODEL, tn), lambda i, j: (0, j))],
        out_specs=pl.BlockSpec((tm, tn), lambda i, j: (i, j)),
        out_shape=jax.ShapeDtypeStruct((n_tok, n_out), F32),
        scratch_shapes=[pltpu.VMEM((tm, D_MODEL), BF16)],
        compiler_params=_cparams(("parallel", "arbitrary")),
        name="in_proj",
    )(x, mod_l, g, w)


def _shift_rows(z, k, fill_first):
    return jnp.concatenate([fill_first, z[: z.shape[0] - k]], axis=0)


def _prep_kernel(z_ref, zp_ref, zn_ref, mu_ref, wup_ref, aup_ref, gup_ref, w0_ref, a0_ref, kk_ref, ka_ref,
                 rk_ref, e_ref, et_ref,
                 r_ref, v_ref, nk_ref, g_ref, bv_ref, lw_ref, kd_ref, bd_ref, zs_scr,
                 *, n_ctx_tiles, lat_tiles_per_seq):
    i = pl.program_id(0)
    tm = z_ref.shape[0]
    z = z_ref[...]
    row = lax.broadcasted_iota(jnp.int32, z.shape, 0)
    lane = lax.broadcasted_iota(jnp.int32, z.shape, 1)
    prev1 = jnp.where(row == 0, 0.0, pltpu.roll(z, 1, 0))
    next1 = jnp.where(row == tm - 1, 0.0, pltpu.roll(z, tm - 1, 0))

    @pl.when(i < n_ctx_tiles)
    def _():
        zs_scr[...] = jnp.where(lane % 2 == 0, prev1, next1)

    @pl.when(i >= n_ctx_tiles)
    def _():
        j = (i - n_ctx_tiles) % lat_tiles_per_seq
        col = row % GRID_W
        left = jnp.where(col == 0, 0.0, prev1)
        right = jnp.where(col == GRID_W - 1, 0.0, next1)
        up_halo = jnp.where(j == 0, 0.0, zp_ref[...])
        dn_halo = jnp.where(j == lat_tiles_per_seq - 1, 0.0, zn_ref[...])
        up = jnp.concatenate([up_halo, z[: tm - GRID_W]], axis=0)
        down = jnp.concatenate([z[GRID_W:], dn_halo], axis=0)
        m = lane % 4
        zs_scr[...] = jnp.where(m == 0, left, jnp.where(m == 1, right, jnp.where(m == 2, up, down)))

    zs = z + mu_ref[...] * (zs_scr[...] - z)
    d = D_MODEL
    r = zs[:, 0:d]
    k = zs[:, d:2 * d]
    v = zs[:, 2 * d:3 * d]
    wd = zs[:, 3 * d:3 * d + 2 * LORA]
    ad = zs[:, 3 * d + 2 * LORA:3 * d + 4 * LORA]
    gd = zs[:, 3 * d + 4 * LORA:]
    r_ref[...] = r
    v_ref[...] = v
    g_ref[...] = _dot(jax.nn.sigmoid(gd).astype(BF16), gup_ref[...])
    w_logit = w0_ref[...] + _dot(jnp.tanh(wd).astype(BF16), wup_ref[...])
    a_all = jax.nn.sigmoid(a0_ref[...] + _dot(ad.astype(BF16), aup_ref[...]))
    kk = k * kk_ref[...]
    kk = kk * lax.rsqrt(_seg_sum(kk * kk, e_ref, et_ref) + 1e-12)
    nk_ref[...] = -kk
    ka = ka_ref[...]
    kd_sum = None
    for dr in range(2):
        a = a_all[:, dr * d:(dr + 1) * d]
        lw_ref[dr] = -DECAY_SCALE * jax.nn.sigmoid(w_logit[:, dr * d:(dr + 1) * d])
        kd = k * (1.0 + (a - 1.0) * ka)
        kd_ref[dr] = kd
        bd_ref[dr] = a * kk
        kd_sum = kd if kd_sum is None else kd_sum + kd
    bv_ref[...] = _seg_sum(kd_sum * rk_ref[...] * r, e_ref, et_ref) * v


def _prep(z_rwkv, lp, n_ctx_tok, t_lat):
    n_tok = z_rwkv.shape[0]
    tm = TM_TOK
    hb = tm // GRID_W
    n_hblk = n_tok // GRID_W
    tok_spec = pl.BlockSpec((tm, D_MODEL), lambda i: (i, 0))
    dir_spec = pl.BlockSpec((2, tm, D_MODEL), lambda i: (0, i, 0))

    def const(shape):
        return pl.BlockSpec(shape, lambda i: (0,) * len(shape))

    kern = functools.partial(_prep_kernel, n_ctx_tiles=n_ctx_tok // tm, lat_tiles_per_seq=t_lat // tm)
    tok_shape = jax.ShapeDtypeStruct((n_tok, D_MODEL), F32)
    dir_shape = jax.ShapeDtypeStruct((2, n_tok, D_MODEL), F32)
    return pl.pallas_call(
        kern,
        grid=(n_tok // tm,),
        in_specs=[pl.BlockSpec((tm, C_RWKV), lambda i: (i, 0)),
                  pl.BlockSpec((GRID_W, C_RWKV), lambda i: (jnp.maximum(i * hb - 1, 0), 0)),
                  pl.BlockSpec((GRID_W, C_RWKV), lambda i: (jnp.minimum((i + 1) * hb, n_hblk - 1), 0)),
                  const((1, C_RWKV)), const((2 * LORA, 2 * D_MODEL)), const((2 * LORA, 2 * D_MODEL)),
                  const((G_LORA, D_MODEL)), const((1, 2 * D_MODEL)), const((1, 2 * D_MODEL)),
                  const((1, D_MODEL)), const((1, D_MODEL)), const((1, D_MODEL)),
                  const((D_MODEL, LANE)), const((LANE, D_MODEL))],
        out_specs=[tok_spec, tok_spec, tok_spec, tok_spec, tok_spec, dir_spec, dir_spec, dir_spec],
        out_shape=[tok_shape] * 5 + [dir_shape] * 3,
        scratch_shapes=[pltpu.VMEM((tm, C_RWKV), F32)],
        compiler_params=_cparams(("parallel",)),
        name="rwkv_prep",
    )(z_rwkv, z_rwkv, z_rwkv, lp["mu"], lp["wup"], lp["aup"], lp["gup"], lp["w0"], lp["a0"], lp["k_k"],
      lp["k_a"], lp["r_k"], lp["e"], lp["et"])


def _mm(a, b, dims=_NN, passes=3):
    out = _dot(a[0], b[0], dims)
    if passes == 3:
        out = out + (_dot(a[0], b[1], dims) + _dot(a[1], b[0], dims))
    return out


def _cols(p, sl):
    return (p[0][:, sl], p[1][:, sl])


def _rows(p, sl):
    return (p[0][sl], p[1][sl])


def _cat(ps, axis):
    return (jnp.concatenate([p[0] for p in ps], axis=axis), jnp.concatenate([p[1] for p in ps], axis=axis))


def _scan_kernel(blk_ref, first_ref, seq_ref, *refs):
    del blk_ref, seq_ref
    in_refs = (refs[0:6], refs[6:12])
    h0_ref, y_refs, hT_ref, s_scr = refs[12], refs[13:15], refs[15], refs[16]
    s = pl.program_id(0)
    n = in_refs[0][0].shape[1]
    heads = range(in_refs[0][0].shape[2] // HEAD)
    hs = [slice(h * HEAD, (h + 1) * HEAD) for h in heads]
    lo_half = slice(0, n)
    hi_half = slice(n, 2 * n)

    @pl.when(first_ref[s] == 1)
    def _():
        s_scr[...] = h0_ref[0]

    row = lax.broadcasted_iota(jnp.int32, (n, n), 0)
    col = lax.broadcasted_iota(jnp.int32, (n, n), 1)
    row2 = lax.broadcasted_iota(jnp.int32, (n, 2 * n), 0)
    lane2 = lax.broadcasted_iota(jnp.int32, (n, 2 * n), 1)
    left = lane2 < n
    col2 = jnp.where(left, lane2, lane2 - n)
    eye_right = jnp.where(jnp.logical_and(jnp.logical_not(left), row2 == col2), 1.0, 0.0)

    incl2, strict2, nr, bk, bkh, vs, v, g_tot = [], [], [], [], [], [], [], []
    for dr in range(2):
        lw_ref, kd_ref, bd_ref, r_ref, v_ref, nk_ref = in_refs[dr]
        sign = 1 - 2 * dr
        tri = ((row - col) * sign >= 0).astype(BF16)
        incl2.append((row2 - col2) * sign >= 0)
        strict2.append((row2 - col2) * sign > 0)
        lw = lw_ref[0]
        lhi, llo = _split(lw)
        cum = _dot(tri, lhi) + _dot(tri, llo)
        tot = cum[n - 1:n, :] if dr == 0 else cum[0:1, :]
        g_inv = jnp.exp(-cum)
        g_rest = jnp.exp(tot - cum)
        g_tot.append(jnp.exp(tot))
        kd = kd_ref[0]
        bd = bd_ref[0]
        v.append(v_ref[...])
        nr.append(_split(jnp.concatenate([nk_ref[...] * jnp.exp(cum - lw), r_ref[...] * jnp.exp(cum)], axis=0)))
        bk.append(_split(jnp.concatenate([bd * g_inv, kd * g_inv], axis=0)))
        bkh.append(_split(jnp.concatenate([bd * g_rest, kd * g_rest], axis=0)))
        vs.append(_split(v[dr]))

    units = [(dr, h) for h in heads for dr in range(2)]
    ps = SCAN_PASSES
    aa = [_mm(_cols(nr[d], hs[h]), _cols(bk[d], hs[h]), _NT, ps["aa"]) for d, h in units]
    top = [jnp.where(strict2[d], a[:n], 0.0) for a, (d, h) in zip(aa, units)]
    bot = [_split(jnp.where(incl2[d], a[n:], 0.0)) for a, (d, h) in zip(aa, units)]
    tops = [_split(t) for t in top]
    x = [_mm(_cols(tp, hi_half), _cols(vs[d], hs[h]), _NN, ps["x"]) for tp, (d, h) in zip(tops, units)]
    shift = INV_BASE.bit_length() - 1
    same_base = jnp.right_shift(row2, shift) == jnp.right_shift(col2, shift)
    slab = [jnp.where(left, jnp.where(same_base, t, 0.0), eye_right) for t in top]
    m = 1
    while m < INV_BASE:
        sp = [_split(sb) for sb in slab]
        slab = [_mm(_cols(p, lo_half), p, _NN, ps["neu"]) + jnp.where(left, 0.0, sb) for p, sb in zip(sp, slab)]
        m *= 2
    tinv = [sb[:, hi_half] for sb in slab]
    a_sq = [t[:, lo_half] for t in top]
    b = INV_BASE
    while b < n:
        sb_, s2b = b.bit_length() - 1, b.bit_length()
        off = jnp.logical_and(jnp.right_shift(row, s2b) == jnp.right_shift(col, s2b),
                              jnp.right_shift(row, sb_) != jnp.right_shift(col, sb_))
        ts = [_split(t) for t in tinv]
        z = [_mm(_split(jnp.where(off, a, 0.0)), t, _NN, ps["neu"]) for a, t in zip(a_sq, ts)]
        tinv = [t + _mm(tp, _split(zz), _NN, ps["neu"]) for t, tp, zz in zip(tinv, ts, z)]
        b *= 2
    rhs = [_cat([_rows(_cols(nr[d], hs[h]), lo_half), _split(xh)], axis=1) for (d, h), xh in zip(units, x)]
    wu = [_mm(_split(t), q, _NN, ps["wu"]) for t, q in zip(tinv, rhs)]
    st = [s_scr[d, h] for d, h in units]
    gs = [_mm(_cat([_cols(_split(w), lo_half), _rows(_cols(nr[d], hs[h]), hi_half)], axis=0), _split(sh), _NT,
              ps["gs"]) for w, (d, h), sh in zip(wu, units, st)]
    uv = [_split(jnp.concatenate([g[:n] + w[:, hi_half], v[d][:, hs[h]]], axis=0))
          for g, w, (d, h) in zip(gs, wu, units)]
    for i, (d, h) in enumerate(units):
        y_refs[d][:, hs[h]] = gs[i][n:] + _mm(bot[i], uv[i], _NN, ps["y"])
    for i, (d, h) in enumerate(units):
        s_scr[d, h] = st[i] * g_tot[d][:, hs[h]] + _mm(uv[i], _cols(bkh[d], hs[h]), _TN, ps["up"])

    hT_ref[0] = s_scr[...]


def _scan_tables(seqs):
    blk = [[], []]
    first, seq = [], []
    sid = 0
    for tok0, nb, t in seqs:
        nc = t // SCAN_L
        for b in range(nb):
            base = (tok0 + b * t) // SCAN_L
            for c in range(nc):
                blk[0].append(base + c)
                blk[1].append(base + nc - 1 - c)
                first.append(1 if c == 0 else 0)
                seq.append(sid)
            sid += 1
    return (jnp.asarray(np.array(blk, np.int32).reshape(-1)), jnp.asarray(np.array(first, np.int32)),
            jnp.asarray(np.array(seq, np.int32)), len(first))


def _scan(lw, kd, bd, r, v, nk, h0, seqs):
    assert SCAN_L == HEAD
    n_tok = r.shape[0]
    n_seq = h0.shape[0]
    blk, first, seq, n_steps = _scan_tables(seqs)

    def dir_spec(d):
        return pl.BlockSpec((1, SCAN_L, D_MODEL), lambda s, blk, first, seq: (d, blk[d * n_steps + s], 0))

    def tok_spec(d):
        return pl.BlockSpec((SCAN_L, D_MODEL), lambda s, blk, first, seq: (blk[d * n_steps + s], 0))

    st_spec = pl.BlockSpec((1, 2, N_HEAD, HEAD, HEAD), lambda s, blk, first, seq: (seq[s], 0, 0, 0, 0))
    in_specs, args = [], []
    for d in range(2):
        in_specs += [dir_spec(d)] * 3 + [tok_spec(d)] * 3
        args += [lw, kd, bd, r, v, nk]
    tok_shape = jax.ShapeDtypeStruct((n_tok, D_MODEL), F32)
    return pl.pallas_call(
        _scan_kernel,
        grid_spec=pltpu.PrefetchScalarGridSpec(
            num_scalar_prefetch=3,
            grid=(n_steps,),
            in_specs=in_specs + [st_spec],
            out_specs=[tok_spec(0), tok_spec(1), st_spec],
            scratch_shapes=[pltpu.VMEM((2, N_HEAD, HEAD, HEAD), F32)]),
        out_shape=[tok_shape, tok_shape, jax.ShapeDtypeStruct((n_seq, 2, N_HEAD, HEAD, HEAD), F32)],
        compiler_params=_cparams(("arbitrary",)),
        name="rwkv_scan",
    )(blk, first, seq, *args, h0)


def _post_kernel(y0_ref, y1_ref, bv_ref, g_ref, lg_ref, lb_ref, e_ref, et_ref, w_ref, o_ref):
    y = y0_ref[...] + y1_ref[...]
    mu = _seg_sum(y, e_ref, et_ref) * (1.0 / HEAD)
    yc = y - mu
    var = _seg_sum(yc * yc, e_ref, et_ref) * (1.0 / HEAD)
    yn = yc * lax.rsqrt(var + GN_EPS) * lg_ref[...] + lb_ref[...]
    o_ref[...] = _dot(((yn + bv_ref[...]) * g_ref[...]).astype(BF16), w_ref[...])


def _post(y0, y1, bv, g, lp):
    n_tok = bv.shape[0]
    tm = TM_TOK
    tok_spec = pl.BlockSpec((tm, D_MODEL), lambda i: (i, 0))

    def const(shape):
        return pl.BlockSpec(shape, lambda i: (0,) * len(shape))

    return pl.pallas_call(
        _post_kernel,
        grid=(n_tok // tm,),
        in_specs=[tok_spec, tok_spec, tok_spec, tok_spec, const((1, D_MODEL)), const((1, D_MODEL)),
                  const((D_MODEL, LANE)), const((LANE, D_MODEL)), const((D_MODEL, D_MODEL))],
        out_specs=tok_spec,
        out_shape=jax.ShapeDtypeStruct((n_tok, D_MODEL), F32),
        compiler_params=_cparams(("parallel",)),
        name="rwkv_post",
    )(y0, y1, bv, g, lp["lnx_g"], lp["lnx_b"], lp["e"], lp["et"], lp["w_branch_a"])


def _cmlp_kernel(zu_ref, zv_ref, lg_ref, ws_ref, bs_ref, w_ref, o_ref, y_scr):
    v = jax.nn.gelu(zv_ref[...])
    mu = jnp.mean(v, axis=-1, keepdims=True)
    vc = v - mu
    var = jnp.mean(vc * vc, axis=-1, keepdims=True)
    vn = (vc * lax.rsqrt(var + EPS) * lg_ref[...]).astype(BF16)
    u = jax.nn.gelu(zu_ref[...])
    for c in range(zu_ref.shape[0] // CHUNK):
        rows = slice(c * CHUNK, (c + 1) * CHUNK)
        for h in range(H_B):
            cols = slice(h * HEAD_B, (h + 1) * HEAD_B)
            s = _dot(ws_ref[h], vn[rows, cols]) + bs_ref[:, cols]
            y_scr[rows, cols] = (u[rows, cols] * s).astype(BF16)
    o_ref[...] = _dot(y_scr[...], w_ref[...])


def _cmlp(z_rest, lp):
    n_tok = z_rest.shape[0]
    tm = TM_TOK

    def const(shape):
        return pl.BlockSpec(shape, lambda i: (0,) * len(shape))

    return pl.pallas_call(
        _cmlp_kernel,
        grid=(n_tok // tm,),
        in_specs=[pl.BlockSpec((tm, D_MODEL), lambda i: (i, 0)),
                  pl.BlockSpec((tm, D_MODEL), lambda i: (i, 1)),
                  const((1, D_MODEL)), const((H_B, CHUNK, CHUNK)), const((CHUNK, D_MODEL)),
                  const((D_MODEL, D_MODEL))],
        out_specs=pl.BlockSpec((tm, D_MODEL), lambda i: (i, 0)),
        out_shape=jax.ShapeDtypeStruct((n_tok, D_MODEL), F32),
        scratch_shapes=[pltpu.VMEM((tm, D_MODEL), BF16)],
        compiler_params=_cparams(("parallel",)),
        name="chunk_mlp",
    )(z_rest, z_rest, lp["ln_v_g"], lp["w_s"], lp["b_s"], lp["w_branch_b"])


def _ffn_kernel(x_ref, ya_ref, yb_ref, ga_ref, gb_ref, mod_ref, g2_ref, wo_ref, w1_ref, w2_ref, fg_ref,
                o_ref, *maybe_final, final):
    m = mod_ref[0]
    mixed = jax.nn.sigmoid(ga_ref[...]) * ya_ref[...] + jax.nn.sigmoid(gb_ref[...]) * yb_ref[...]
    x = x_ref[...] + m[2:3] * _dot(mixed.astype(BF16), wo_ref[...])
    h2 = _rms_mod(x, g2_ref[...], m[3:4], m[4:5]).astype(BF16)
    acc = jnp.zeros(x.shape, F32)
    ff_chunk = D_MODEL
    for c in range(D_FF // ff_chunk):
        cols = slice(c * ff_chunk, (c + 1) * ff_chunk)
        hid = jnp.square(jnp.maximum(_dot(h2, w1_ref[:, cols]), 0.0)).astype(BF16)
        acc = acc + _dot(hid, w2_ref[cols, :])
    x = x + m[5:6] * acc
    o_ref[...] = x
    if final:
        yf = x * lax.rsqrt(jnp.mean(x * x, axis=-1, keepdims=True) + EPS) * fg_ref[...]
        maybe_final[0][...] = yf


def _ffn(x, ya, yb, z_rest, mod_l, lp, final_g, mod_row, final):
    n_tok = x.shape[0]
    tm = TM_TOK
    tok_spec = pl.BlockSpec((tm, D_MODEL), lambda i: (i, 0))

    def const(shape):
        return pl.BlockSpec(shape, lambda i: (0,) * len(shape), pipeline_mode=pl.Buffered(1))

    tok_shape = jax.ShapeDtypeStruct((n_tok, D_MODEL), F32)
    return pl.pallas_call(
        functools.partial(_ffn_kernel, final=final),
        grid=(n_tok // tm,),
        in_specs=[tok_spec, tok_spec, tok_spec,
                  pl.BlockSpec((tm, D_MODEL), lambda i: (i, 2)),
                  pl.BlockSpec((tm, D_MODEL), lambda i: (i, 3)),
                  pl.BlockSpec((1, N_MOD, D_MODEL), lambda i: (mod_row(i), 0, 0)),
                  const((1, D_MODEL)), const((D_MODEL, D_MODEL)), const((D_MODEL, D_FF)),
                  const((D_FF, D_MODEL)), const((1, D_MODEL))],
        out_specs=[tok_spec, tok_spec] if final else [tok_spec],
        out_shape=[tok_shape, tok_shape] if final else [tok_shape],
        compiler_params=_cparams(("parallel",)),
        name="mix_ffn",
    )(x, ya, yb, z_rest, z_rest, mod_l, lp["norm2_g"], lp["w_out"], lp["w1"], lp["w2"], final_g)


def _block_diag2(m):
    z = jnp.zeros_like(m[0])
    return jnp.concatenate([jnp.concatenate([m[0], z], axis=1), jnp.concatenate([z, m[1]], axis=1)], axis=0)


def _layer_params(l, w_in, mu_shift, w0, w_up, a0, a_up, g_up, k_k, k_a, r_k, lnx_g, lnx_b, w_branch_a,
                  ln_v_g, w_s, b_s, w_branch_b, w_out, w1, w2, norm1_g, norm2_g):
    head_of = np.arange(D_MODEL) // HEAD
    e = (head_of[:, None] == np.arange(LANE)[None, :]).astype(np.float32)
    row = lambda a: a.reshape(1, -1)
    return dict(
        w_in_rwkv=w_in[l][:, :C_RWKV].astype(BF16), w_in_rest=w_in[l][:, C_RWKV:].astype(BF16),
        mu=row(mu_shift[l]), wup=_block_diag2(w_up[l]).astype(BF16), aup=_block_diag2(a_up[l]).astype(BF16),
        gup=g_up[l].astype(BF16), w0=row(w0[l]), a0=row(a0[l]), k_k=row(k_k[l]), k_a=row(k_a[l]),
        r_k=row(r_k[l]), lnx_g=row(lnx_g[l]), lnx_b=row(lnx_b[l]), w_branch_a=w_branch_a[l].astype(BF16),
        ln_v_g=row(ln_v_g[l]), w_s=w_s[l].astype(BF16), b_s=jnp.repeat(b_s[l].T, HEAD_B, axis=1),
        w_branch_b=w_branch_b[l].astype(BF16), w_out=w_out[l].astype(BF16), w1=w1[l].astype(BF16),
        w2=w2[l].astype(BF16), norm1_g=row(norm1_g[l]), norm2_g=row(norm2_g[l]),
        e=jnp.asarray(e, BF16), et=jnp.asarray(e.T, BF16))


def kernel(x_prompt, x_sample, state_rwkv, c, c_ctx, w_ada, b_ada, norm1_g, norm2_g, w_in, mu_shift, w0, w_up,
           a0, a_up, g_up, k_k, k_a, r_k, lnx_g, lnx_b, w_branch_a, ln_v_g, w_s, b_s, w_branch_b, w_out, w1, w2,
           final_g):
    b_ctx, t_ctx, _ = x_prompt.shape
    b_lat, t_lat, _ = x_sample.shape
    depth = w_in.shape[0]
    n_ctx = b_ctx * t_ctx
    n_lat = b_lat * t_lat
    assert t_ctx == TM_TOK and t_lat % TM_TOK == 0 and t_lat % GRID_W == 0

    x = jnp.concatenate([x_prompt.reshape(n_ctx, D_MODEL), x_sample.reshape(n_lat, D_MODEL)], axis=0)
    cond = jnp.concatenate([c_ctx[None, :], c], axis=0)
    mod = _modulation(cond, w_ada, b_ada).reshape(depth, 1 + b_lat, N_MOD, D_MODEL)
    seqs = [(0, b_ctx, t_ctx), (n_ctx, b_lat, t_lat)]
    s_zero = jnp.zeros((b_ctx, 2, N_HEAD, HEAD, HEAD), F32)
    final_row = final_g.reshape(1, D_MODEL)

    states = []
    y_final = None
    for l in range(depth):
        lp = _layer_params(l, w_in, mu_shift, w0, w_up, a0, a_up, g_up, k_k, k_a, r_k, lnx_g, lnx_b,
                           w_branch_a, ln_v_g, w_s, b_s, w_branch_b, w_out, w1, w2, norm1_g, norm2_g)
        tm_proj = math.gcd(TM_PROJ, n_ctx, t_lat)
        row_proj = _mod_row_map(n_ctx, t_lat, tm_proj)
        row_tm = _mod_row_map(n_ctx, t_lat, TM_TOK)
        z_rwkv = _in_proj(x, mod[l], lp["norm1_g"], lp["w_in_rwkv"], 1152, row_proj, tm_proj)
        z_rest = _in_proj(x, mod[l], lp["norm1_g"], lp["w_in_rest"], 1024, row_proj, tm_proj)
        r, v, nk, g, bv, lw, kd, bd = _prep(z_rwkv, lp, n_ctx, t_lat)
        h0 = jnp.concatenate([s_zero, state_rwkv[:, l]], axis=0)
        y0, y1, h_fin = _scan(lw, kd, bd, r, v, nk, h0, seqs)
        states.append(h_fin[:b_ctx])
        ya = _post(y0, y1, bv, g, lp)
        yb = _cmlp(z_rest, lp)
        outs = _ffn(x, ya, yb, z_rest, mod[l], lp, final_row, row_tm, final=(l == depth - 1))
        x = outs[0]
        if l == depth - 1:
            y_final = outs[1]

    y_prompt = y_final[:n_ctx].reshape(b_ctx, t_ctx, D_MODEL)
    y_sample = y_final[n_ctx:].reshape(b_lat, t_lat, D_MODEL)
    return (y_prompt, y_sample, jnp.stack(states, axis=1))
```

```python
import functools
import math

import numpy as np
import jax
import jax.numpy as jnp
from jax import lax
from jax.experimental import pallas as pl
from jax.experimental.pallas import tpu as pltpu

F32 = jnp.float32
BF16 = jnp.bfloat16

D_MODEL = 1024
HEAD = 64
N_HEAD = D_MODEL // HEAD
LORA = 64
G_LORA = 128
C_RWKV = 3 * D_MODEL + 4 * LORA + G_LORA
D_REST = 4 * D_MODEL
D_FF = 4 * D_MODEL
GRID_W = 64
CHUNK = 128
H_B = 8
HEAD_B = D_MODEL // H_B
N_MOD = 6
EPS = 1e-6
GN_EPS = 64e-5
DECAY_SCALE = math.exp(-0.5)

SCAN_L = 64
LANE = 128
SCAN_PASSES = dict(aa=1, x=1, neu=1, wu=1, gs=1, y=1, up=1)
INV_BASE = 8
TM_TOK = 256
TM_PROJ = 2048
VMEM_LIMIT = 56 * 1024 * 1024


def _cparams(sem):
    return pltpu.CompilerParams(dimension_semantics=sem, vmem_limit_bytes=VMEM_LIMIT)


def _split(x):
    hi = x.astype(BF16)
    lo = (x - hi.astype(F32)).astype(BF16)
    return hi, lo


def _dot(a, b, dims=(((1,), (0,)), ((), ()))):
    return lax.dot_general(a, b, dims, preferred_element_type=F32)


_NN = (((1,), (0,)), ((), ()))
_NT = (((1,), (1,)), ((), ()))
_TN = (((0,), (0,)), ((), ()))


def _dot3(a, b, dims=_NN):
    ah, al = _split(a)
    bh, bl = _split(b)
    return _dot(ah, bh, dims) + (_dot(ah, bl, dims) + _dot(al, bh, dims))


def _seg_sum(x, e_ref, et_ref):
    hi, lo = _split(x)
    s = _dot(hi, e_ref[...]) + _dot(lo, e_ref[...])
    shi, slo = _split(s)
    return _dot(shi, et_ref[...]) + _dot(slo, et_ref[...])


def _mod_kernel(c_ref, w_ref, b_ref, o_ref):
    cond = c_ref[...]
    o_ref[0] = _dot(jax.nn.silu(cond).astype(BF16), w_ref[0].astype(BF16)) + b_ref[0]


def _modulation(cond, w_ada, b_ada):
    depth = w_ada.shape[0]
    n = cond.shape[0]
    tn = 1536
    return pl.pallas_call(
        _mod_kernel,
        grid=(depth, (N_MOD * D_MODEL) // tn),
        in_specs=[pl.BlockSpec((n, D_MODEL), lambda l, j: (0, 0)),
                  pl.BlockSpec((1, D_MODEL, tn), lambda l, j: (l, 0, j)),
                  pl.BlockSpec((1, 1, tn), lambda l, j: (l, 0, j))],
        out_specs=pl.BlockSpec((1, n, tn), lambda l, j: (l, 0, j)),
        out_shape=jax.ShapeDtypeStruct((depth, n, N_MOD * D_MODEL), F32),
        compiler_params=_cparams(("parallel", "parallel")),
        name="modulation",
    )(cond, w_ada, b_ada.reshape(depth, 1, N_MOD * D_MODEL))


def _mod_row_map(n_ctx_tok, t_lat, tm):
    def row(i):
        tok = i * tm
        return jnp.where(tok < n_ctx_tok, 0, 1 + (tok - n_ctx_tok) // t_lat)
    return row


def _rms_mod(x, g, shift, scale):
    y = x * lax.rsqrt(jnp.mean(x * x, axis=-1, keepdims=True) + EPS) * g
    return y * (1.0 + scale) + shift


def _in_proj_kernel(x_ref, mod_ref, g_ref, w_ref, o_ref, h_scr):
    @pl.when(pl.program_id(1) == 0)
    def _():
        m = mod_ref[0]
        h_scr[...] = _rms_mod(x_ref[...], g_ref[...], m[0:1], m[1:2]).astype(BF16)

    o_ref[...] = _dot(h_scr[...], w_ref[...])


def _in_proj(x, mod_l, g, w, tn, mod_row, tm):
    n_tok = x.shape[0]
    n_out = w.shape[1]
    return pl.pallas_call(
        _in_proj_kernel,
        grid=(n_tok // tm, n_out // tn),
        in_specs=[pl.BlockSpec((tm, D_MODEL), lambda i, j: (i, 0)),
                  pl.BlockSpec((1, N_MOD, D_MODEL), lambda i, j: (mod_row(i), 0, 0)),
                  pl.BlockSpec((1, D_MODEL), lambda i, j: (0, 0)),
                  pl.BlockSpec((D_MODEL, tn), lambda i, j: (0, j))],
        out_specs=pl.BlockSpec((tm, tn), lambda i, j: (i, j)),
        out_shape=jax.ShapeDtypeStruct((n_tok, n_out), F32),
        scratch_shapes=[pltpu.VMEM((tm, D_MODEL), BF16)],
        compiler_params=_cparams(("parallel", "arbitrary")),
        name="in_proj",
    )(x, mod_l, g, w)


def _shift_rows(z, k, fill_first):
    return jnp.concatenate([fill_first, z[: z.shape[0] - k]], axis=0)


def _prep_kernel(z_ref, zp_ref, zn_ref, mu_ref, wup_ref, aup_ref, gup_ref, w0_ref, a0_ref, kk_ref, ka_ref,
                 rk_ref, e_ref, et_ref,
                 r_ref, v_ref, nk_ref, g_ref, bv_ref, lw_ref, kd_ref, bd_ref, zs_scr,
                 *, n_ctx_tiles, lat_tiles_per_seq):
    i = pl.program_id(0)
    tm = z_ref.shape[0]
    z = z_ref[...]
    row = lax.broadcasted_iota(jnp.int32, z.shape, 0)
    lane = lax.broadcasted_iota(jnp.int32, z.shape, 1)
    prev1 = jnp.where(row == 0, 0.0, pltpu.roll(z, 1, 0))
    next1 = jnp.where(row == tm - 1, 0.0, pltpu.roll(z, tm - 1, 0))

    @pl.when(i < n_ctx_tiles)
    def _():
        zs_scr[...] = jnp.where(lane % 2 == 0, prev1, next1)

    @pl.when(i >= n_ctx_tiles)
    def _():
        j = (i - n_ctx_tiles) % lat_tiles_per_seq
        col = row % GRID_W
        left = jnp.where(col == 0, 0.0, prev1)
        right = jnp.where(col == GRID_W - 1, 0.0, next1)
        up_halo = jnp.where(j == 0, 0.0, zp_ref[...])
        dn_halo = jnp.where(j == lat_tiles_per_seq - 1, 0.0, zn_ref[...])
        up = jnp.concatenate([up_halo, z[: tm - GRID_W]], axis=0)
        down = jnp.concatenate([z[GRID_W:], dn_halo], axis=0)
        m = lane % 4
        zs_scr[...] = jnp.where(m == 0, left, jnp.where(m == 1, right, jnp.where(m == 2, up, down)))

    zs = z + mu_ref[...] * (zs_scr[...] - z)
    d = D_MODEL
    r = zs[:, 0:d]
    k = zs[:, d:2 * d]
    v = zs[:, 2 * d:3 * d]
    wd = zs[:, 3 * d:3 * d + 2 * LORA]
    ad = zs[:, 3 * d + 2 * LORA:3 * d + 4 * LORA]
    gd = zs[:, 3 * d + 4 * LORA:]
    r_ref[...] = r.astype(BF16)
    v_ref[...] = v.astype(BF16)
    g_ref[...] = _dot(jax.nn.sigmoid(gd).astype(BF16), gup_ref[...])
    w_logit = w0_ref[...] + _dot(jnp.tanh(wd).astype(BF16), wup_ref[...])
    a_all = jax.nn.sigmoid(a0_ref[...] + _dot(ad.astype(BF16), aup_ref[...]))
    kk = k * kk_ref[...]
    kk = kk * lax.rsqrt(_seg_sum(kk * kk, e_ref, et_ref) + 1e-12)
    nk_ref[...] = (-kk).astype(BF16)
    ka = ka_ref[...]
    kd_sum = None
    for dr in range(2):
        a = a_all[:, dr * d:(dr + 1) * d]
        lw_ref[dr] = -DECAY_SCALE * jax.nn.sigmoid(w_logit[:, dr * d:(dr + 1) * d])
        kd = k * (1.0 + (a - 1.0) * ka)
        kd_ref[dr] = kd.astype(BF16)
        bd_ref[dr] = (a * kk).astype(BF16)
        kd_sum = kd if kd_sum is None else kd_sum + kd
    bv_ref[...] = _seg_sum(kd_sum * rk_ref[...] * r, e_ref, et_ref) * v


def _prep(z_rwkv, lp, n_ctx_tok, t_lat):
    n_tok = z_rwkv.shape[0]
    tm = TM_TOK
    hb = tm // GRID_W
    n_hblk = n_tok // GRID_W
    tok_spec = pl.BlockSpec((tm, D_MODEL), lambda i: (i, 0))
    dir_spec = pl.BlockSpec((2, tm, D_MODEL), lambda i: (0, i, 0))

    def const(shape):
        return pl.BlockSpec(shape, lambda i: (0,) * len(shape))

    n_ctx_tiles = n_ctx_tok // tm
    kern = functools.partial(_prep_kernel, n_ctx_tiles=n_ctx_tiles, lat_tiles_per_seq=t_lat // tm)
    tok_shape = jax.ShapeDtypeStruct((n_tok, D_MODEL), F32)
    dir_shape = jax.ShapeDtypeStruct((2, n_tok, D_MODEL), F32)
    tok_bf16 = jax.ShapeDtypeStruct((n_tok, D_MODEL), BF16)
    dir_bf16 = jax.ShapeDtypeStruct((2, n_tok, D_MODEL), BF16)
    return pl.pallas_call(
        kern,
        grid=(n_tok // tm,),
        in_specs=[pl.BlockSpec((tm, C_RWKV), lambda i: (i, 0)),
                  pl.BlockSpec((GRID_W, C_RWKV), lambda i: (jnp.where(i < n_ctx_tiles, 0, i * hb - 1), 0)),
                  pl.BlockSpec((GRID_W, C_RWKV),
                               lambda i: (jnp.where(i < n_ctx_tiles, 0, jnp.minimum((i + 1) * hb, n_hblk - 1)), 0)),
                  const((1, C_RWKV)), const((2 * LORA, 2 * D_MODEL)), const((2 * LORA, 2 * D_MODEL)),
                  const((G_LORA, D_MODEL)), const((1, 2 * D_MODEL)), const((1, 2 * D_MODEL)),
                  const((1, D_MODEL)), const((1, D_MODEL)), const((1, D_MODEL)),
                  const((D_MODEL, LANE)), const((LANE, D_MODEL))],
        out_specs=[tok_spec, tok_spec, tok_spec, tok_spec, tok_spec, dir_spec, dir_spec, dir_spec],
        out_shape=[tok_bf16] * 3 + [tok_shape] * 2 + [dir_shape, dir_bf16, dir_bf16],
        scratch_shapes=[pltpu.VMEM((tm, C_RWKV), F32)],
        compiler_params=_cparams(("parallel",)),
        name="rwkv_prep",
    )(z_rwkv, z_rwkv, z_rwkv, lp["mu"], lp["wup"], lp["aup"], lp["gup"], lp["w0"], lp["a0"], lp["k_k"],
      lp["k_a"], lp["r_k"], lp["e"], lp["et"])


def _mm(a, b, dims=_NN, passes=3):
    out = _dot(a[0], b[0], dims)
    if passes == 3:
        out = out + (_dot(a[0], b[1], dims) + _dot(a[1], b[0], dims))
    return out


def _cols(p, sl):
    return (p[0][:, sl], p[1][:, sl])


def _rows(p, sl):
    return (p[0][sl], p[1][sl])


def _cat(ps, axis):
    return (jnp.concatenate([p[0] for p in ps], axis=axis), jnp.concatenate([p[1] for p in ps], axis=axis))


def _scan_kernel(blk_ref, first_ref, seq_ref, *refs):
    del blk_ref, seq_ref
    in_refs = (refs[0:6], refs[6:12])
    h0_ref, y_refs, hT_ref, s_scr = refs[12], refs[13:15], refs[15], refs[16]
    s = pl.program_id(0)
    n = in_refs[0][0].shape[1]
    heads = range(in_refs[0][0].shape[2] // HEAD)
    hs = [slice(h * HEAD, (h + 1) * HEAD) for h in heads]
    lo_half = slice(0, n)
    hi_half = slice(n, 2 * n)

    @pl.when(first_ref[s] == 1)
    def _():
        s_scr[...] = h0_ref[0]

    row = lax.broadcasted_iota(jnp.int32, (n, n), 0)
    col = lax.broadcasted_iota(jnp.int32, (n, n), 1)
    row2 = lax.broadcasted_iota(jnp.int32, (n, 2 * n), 0)
    lane2 = lax.broadcasted_iota(jnp.int32, (n, 2 * n), 1)
    left = lane2 < n
    col2 = jnp.where(left, lane2, lane2 - n)
    eye_right = jnp.where(jnp.logical_and(jnp.logical_not(left), row2 == col2), 1.0, 0.0)

    incl2, strict2, nr, bk, bkh, vs, v, g_tot = [], [], [], [], [], [], [], []
    for dr in range(2):
        lw_ref, kd_ref, bd_ref, r_ref, v_ref, nk_ref = in_refs[dr]
        sign = 1 - 2 * dr
        tri = ((row - col) * sign >= 0).astype(BF16)
        incl2.append((row2 - col2) * sign >= 0)
        strict2.append((row2 - col2) * sign > 0)
        lw = lw_ref[0]
        lhi, llo = _split(lw)
        cum = _dot(tri, lhi) + _dot(tri, llo)
        tot = cum[n - 1:n, :] if dr == 0 else cum[0:1, :]
        g_inv = jnp.exp(-cum)
        g_rest = jnp.exp(tot - cum)
        g_tot.append(jnp.exp(tot))
        kd = kd_ref[0].astype(F32)
        bd = bd_ref[0].astype(F32)
        v.append(v_ref[...].astype(F32))
        nr.append(_split(jnp.concatenate([nk_ref[...].astype(F32) * jnp.exp(cum - lw),
                                          r_ref[...].astype(F32) * jnp.exp(cum)], axis=0)))
        bk.append(_split(jnp.concatenate([bd * g_inv, kd * g_inv], axis=0)))
        bkh.append(_split(jnp.concatenate([bd * g_rest, kd * g_rest], axis=0)))
        vs.append(_split(v[dr]))

    units = [(dr, h) for h in heads for dr in range(2)]
    ps = SCAN_PASSES
    aa = [_mm(_cols(nr[d], hs[h]), _cols(bk[d], hs[h]), _NT, ps["aa"]) for d, h in units]
    top = [jnp.where(strict2[d], a[:n], 0.0) for a, (d, h) in zip(aa, units)]
    bot = [_split(jnp.where(incl2[d], a[n:], 0.0)) for a, (d, h) in zip(aa, units)]
    tops = [_split(t) for t in top]
    x = [_mm(_cols(tp, hi_half), _cols(vs[d], hs[h]), _NN, ps["x"]) for tp, (d, h) in zip(tops, units)]
    shift = INV_BASE.bit_length() - 1
    same_base = jnp.right_shift(row2, shift) == jnp.right_shift(col2, shift)
    slab = [jnp.where(left, jnp.where(same_base, t, 0.0), eye_right) for t in top]
    m = 1
    while m < INV_BASE:
        sp = [_split(sb) for sb in slab]
        slab = [_mm(_cols(p, lo_half), p, _NN, ps["neu"]) + jnp.where(left, 0.0, sb) for p, sb in zip(sp, slab)]
        m *= 2
    tinv = [sb[:, hi_half] for sb in slab]
    a_sq = [t[:, lo_half] for t in top]
    b = INV_BASE
    while b < n:
        sb_, s2b = b.bit_length() - 1, b.bit_length()
        off = jnp.logical_and(jnp.right_shift(row, s2b) == jnp.right_shift(col, s2b),
                              jnp.right_shift(row, sb_) != jnp.right_shift(col, sb_))
        ts = [_split(t) for t in tinv]
        z = [_mm(_split(jnp.where(off, a, 0.0)), t, _NN, ps["neu"]) for a, t in zip(a_sq, ts)]
        tinv = [t + _mm(tp, _split(zz), _NN, ps["neu"]) for t, tp, zz in zip(tinv, ts, z)]
        b *= 2
    rhs = [_cat([_rows(_cols(nr[d], hs[h]), lo_half), _split(xh)], axis=1) for (d, h), xh in zip(units, x)]
    wu = [_mm(_split(t), q, _NN, ps["wu"]) for t, q in zip(tinv, rhs)]
    st = [s_scr[d, h] for d, h in units]
    gs = [_mm(_cat([_cols(_split(w), lo_half), _rows(_cols(nr[d], hs[h]), hi_half)], axis=0), _split(sh), _NT,
              ps["gs"]) for w, (d, h), sh in zip(wu, units, st)]
    uv = [_split(jnp.concatenate([g[:n] + w[:, hi_half], v[d][:, hs[h]]], axis=0))
          for g, w, (d, h) in zip(gs, wu, units)]
    for i, (d, h) in enumerate(units):
        y_refs[d][:, hs[h]] = gs[i][n:] + _mm(bot[i], uv[i], _NN, ps["y"])
    for i, (d, h) in enumerate(units):
        s_scr[d, h] = st[i] * g_tot[d][:, hs[h]] + _mm(uv[i], _cols(bkh[d], hs[h]), _TN, ps["up"])

    hT_ref[0] = s_scr[...]


def _scan_tables(seqs):
    blk = [[], []]
    first, seq = [], []
    sid = 0
    for tok0, nb, t in seqs:
        nc = t // SCAN_L
        for b in range(nb):
            base = (tok0 + b * t) // SCAN_L
            for c in range(nc):
                blk[0].append(base + c)
                blk[1].append(base + nc - 1 - c)
                first.append(1 if c == 0 else 0)
                seq.append(sid)
            sid += 1
    return (jnp.asarray(np.array(blk, np.int32).reshape(-1)), jnp.asarray(np.array(first, np.int32)),
            jnp.asarray(np.array(seq, np.int32)), len(first))


def _scan(lw, kd, bd, r, v, nk, h0, seqs):
    assert SCAN_L == HEAD
    n_tok = r.shape[0]
    n_seq = h0.shape[0]
    blk, first, seq, n_steps = _scan_tables(seqs)

    def dir_spec(d):
        return pl.BlockSpec((1, SCAN_L, D_MODEL), lambda s, blk, first, seq: (d, blk[d * n_steps + s], 0))

    def tok_spec(d):
        return pl.BlockSpec((SCAN_L, D_MODEL), lambda s, blk, first, seq: (blk[d * n_steps + s], 0))

    st_spec = pl.BlockSpec((1, 2, N_HEAD, HEAD, HEAD), lambda s, blk, first, seq: (seq[s], 0, 0, 0, 0))
    in_specs, args = [], []
    for d in range(2):
        in_specs += [dir_spec(d)] * 3 + [tok_spec(d)] * 3
        args += [lw, kd, bd, r, v, nk]
    tok_shape = jax.ShapeDtypeStruct((n_tok, D_MODEL), F32)
    return pl.pallas_call(
        _scan_kernel,
        grid_spec=pltpu.PrefetchScalarGridSpec(
            num_scalar_prefetch=3,
            grid=(n_steps,),
            in_specs=in_specs + [st_spec],
            out_specs=[tok_spec(0), tok_spec(1), st_spec],
            scratch_shapes=[pltpu.VMEM((2, N_HEAD, HEAD, HEAD), F32)]),
        out_shape=[tok_shape, tok_shape, jax.ShapeDtypeStruct((n_seq, 2, N_HEAD, HEAD, HEAD), F32)],
        compiler_params=_cparams(("arbitrary",)),
        name="rwkv_scan",
    )(blk, first, seq, *args, h0)


def _post_kernel(y0_ref, y1_ref, bv_ref, g_ref, lg_ref, lb_ref, e_ref, et_ref, w_ref, o_ref):
    y = y0_ref[...] + y1_ref[...]
    mu = _seg_sum(y, e_ref, et_ref) * (1.0 / HEAD)
    yc = y - mu
    var = _seg_sum(yc * yc, e_ref, et_ref) * (1.0 / HEAD)
    yn = yc * lax.rsqrt(var + GN_EPS) * lg_ref[...] + lb_ref[...]
    o_ref[...] = _dot(((yn + bv_ref[...]) * g_ref[...]).astype(BF16), w_ref[...])


def _post(y0, y1, bv, g, lp):
    n_tok = bv.shape[0]
    tm = TM_TOK
    tok_spec = pl.BlockSpec((tm, D_MODEL), lambda i: (i, 0))

    def const(shape):
        return pl.BlockSpec(shape, lambda i: (0,) * len(shape))

    return pl.pallas_call(
        _post_kernel,
        grid=(n_tok // tm,),
        in_specs=[tok_spec, tok_spec, tok_spec, tok_spec, const((1, D_MODEL)), const((1, D_MODEL)),
                  const((D_MODEL, LANE)), const((LANE, D_MODEL)), const((D_MODEL, D_MODEL))],
        out_specs=tok_spec,
        out_shape=jax.ShapeDtypeStruct((n_tok, D_MODEL), F32),
        compiler_params=_cparams(("parallel",)),
        name="rwkv_post",
    )(y0, y1, bv, g, lp["lnx_g"], lp["lnx_b"], lp["e"], lp["et"], lp["w_branch_a"])


def _cmlp_kernel(zu_ref, zv_ref, lg_ref, ws_ref, bs_ref, w_ref, o_ref, y_scr):
    v = jax.nn.gelu(zv_ref[...])
    mu = jnp.mean(v, axis=-1, keepdims=True)
    vc = v - mu
    var = jnp.mean(vc * vc, axis=-1, keepdims=True)
    vn = (vc * lax.rsqrt(var + EPS) * lg_ref[...]).astype(BF16)
    u = jax.nn.gelu(zu_ref[...])
    for c in range(zu_ref.shape[0] // CHUNK):
        rows = slice(c * CHUNK, (c + 1) * CHUNK)
        for h in range(H_B):
            cols = slice(h * HEAD_B, (h + 1) * HEAD_B)
            s = _dot(ws_ref[h], vn[rows, cols]) + bs_ref[:, cols]
            y_scr[rows, cols] = (u[rows, cols] * s).astype(BF16)
    o_ref[...] = _dot(y_scr[...], w_ref[...])


def _cmlp(z_rest, lp):
    n_tok = z_rest.shape[0]
    tm = TM_TOK

    def const(shape):
        return pl.BlockSpec(shape, lambda i: (0,) * len(shape))

    return pl.pallas_call(
        _cmlp_kernel,
        grid=(n_tok // tm,),
        in_specs=[pl.BlockSpec((tm, D_MODEL), lambda i: (i, 0)),
                  pl.BlockSpec((tm, D_MODEL), lambda i: (i, 1)),
                  const((1, D_MODEL)), const((H_B, CHUNK, CHUNK)), const((CHUNK, D_MODEL)),
                  const((D_MODEL, D_MODEL))],
        out_specs=pl.BlockSpec((tm, D_MODEL), lambda i: (i, 0)),
        out_shape=jax.ShapeDtypeStruct((n_tok, D_MODEL), F32),
        scratch_shapes=[pltpu.VMEM((tm, D_MODEL), BF16)],
        compiler_params=_cparams(("parallel",)),
        name="chunk_mlp",
    )(z_rest, z_rest, lp["ln_v_g"], lp["w_s"], lp["b_s"], lp["w_branch_b"])


def _ffn_kernel(x_ref, ya_ref, yb_ref, ga_ref, gb_ref, mod_ref, g2_ref, wo_ref, w1_ref, w2_ref, fg_ref,
                o_ref, *maybe_final, final):
    m = mod_ref[0]
    mixed = jax.nn.sigmoid(ga_ref[...]) * ya_ref[...] + jax.nn.sigmoid(gb_ref[...]) * yb_ref[...]
    x = x_ref[...] + m[2:3] * _dot(mixed.astype(BF16), wo_ref[...])
    h2 = _rms_mod(x, g2_ref[...], m[3:4], m[4:5]).astype(BF16)
    acc = jnp.zeros(x.shape, F32)
    ff_chunk = D_MODEL
    for c in range(D_FF // ff_chunk):
        cols = slice(c * ff_chunk, (c + 1) * ff_chunk)
        hid = jnp.square(jnp.maximum(_dot(h2, w1_ref[:, cols]), 0.0)).astype(BF16)
        acc = acc + _dot(hid, w2_ref[cols, :])
    x = x + m[5:6] * acc
    o_ref[...] = x
    if final:
        yf = x * lax.rsqrt(jnp.mean(x * x, axis=-1, keepdims=True) + EPS) * fg_ref[...]
        maybe_final[0][...] = yf


def _ffn(x, ya, yb, z_rest, mod_l, lp, final_g, mod_row, final):
    n_tok = x.shape[0]
    tm = TM_TOK
    tok_spec = pl.BlockSpec((tm, D_MODEL), lambda i: (i, 0))

    def const(shape):
        return pl.BlockSpec(shape, lambda i: (0,) * len(shape), pipeline_mode=pl.Buffered(1))

    tok_shape = jax.ShapeDtypeStruct((n_tok, D_MODEL), F32)
    return pl.pallas_call(
        functools.partial(_ffn_kernel, final=final),
        grid=(n_tok // tm,),
        in_specs=[tok_spec, tok_spec, tok_spec,
                  pl.BlockSpec((tm, D_MODEL), lambda i: (i, 2)),
                  pl.BlockSpec((tm, D_MODEL), lambda i: (i, 3)),
                  pl.BlockSpec((1, N_MOD, D_MODEL), lambda i: (mod_row(i), 0, 0)),
                  const((1, D_MODEL)), const((D_MODEL, D_MODEL)), const((D_MODEL, D_FF)),
                  const((D_FF, D_MODEL)), const((1, D_MODEL))],
        out_specs=[tok_spec, tok_spec] if final else [tok_spec],
        out_shape=[tok_shape, tok_shape] if final else [tok_shape],
        compiler_params=_cparams(("parallel",)),
        name="mix_ffn",
    )(x, ya, yb, z_rest, z_rest, mod_l, lp["norm2_g"], lp["w_out"], lp["w1"], lp["w2"], final_g)


def _block_diag2(m):
    z = jnp.zeros_like(m[0])
    return jnp.concatenate([jnp.concatenate([m[0], z], axis=1), jnp.concatenate([z, m[1]], axis=1)], axis=0)


def _layer_params(l, w_in, mu_shift, w0, w_up, a0, a_up, g_up, k_k, k_a, r_k, lnx_g, lnx_b, w_branch_a,
                  ln_v_g, w_s, b_s, w_branch_b, w_out, w1, w2, norm1_g, norm2_g):
    head_of = np.arange(D_MODEL) // HEAD
    e = (head_of[:, None] == np.arange(LANE)[None, :]).astype(np.float32)
    row = lambda a: a.reshape(1, -1)
    return dict(
        w_in_rwkv=w_in[l][:, :C_RWKV].astype(BF16), w_in_rest=w_in[l][:, C_RWKV:].astype(BF16),
        mu=row(mu_shift[l]), wup=_block_diag2(w_up[l]).astype(BF16), aup=_block_diag2(a_up[l]).astype(BF16),
        gup=g_up[l].astype(BF16), w0=row(w0[l]), a0=row(a0[l]), k_k=row(k_k[l]), k_a=row(k_a[l]),
        r_k=row(r_k[l]), lnx_g=row(lnx_g[l]), lnx_b=row(lnx_b[l]), w_branch_a=w_branch_a[l].astype(BF16),
        ln_v_g=row(ln_v_g[l]), w_s=w_s[l].astype(BF16), b_s=jnp.repeat(b_s[l].T, HEAD_B, axis=1),
        w_branch_b=w_branch_b[l].astype(BF16), w_out=w_out[l].astype(BF16), w1=w1[l].astype(BF16),
        w2=w2[l].astype(BF16), norm1_g=row(norm1_g[l]), norm2_g=row(norm2_g[l]),
        e=jnp.asarray(e, BF16), et=jnp.asarray(e.T, BF16))


def kernel(x_prompt, x_sample, state_rwkv, c, c_ctx, w_ada, b_ada, norm1_g, norm2_g, w_in, mu_shift, w0, w_up,
           a0, a_up, g_up, k_k, k_a, r_k, lnx_g, lnx_b, w_branch_a, ln_v_g, w_s, b_s, w_branch_b, w_out, w1, w2,
           final_g):
    b_ctx, t_ctx, _ = x_prompt.shape
    b_lat, t_lat, _ = x_sample.shape
    depth = w_in.shape[0]
    n_ctx = b_ctx * t_ctx
    n_lat = b_lat * t_lat
    assert t_ctx == TM_TOK and t_lat % TM_TOK == 0 and t_lat % GRID_W == 0

    x = jnp.concatenate([x_prompt.reshape(n_ctx, D_MODEL), x_sample.reshape(n_lat, D_MODEL)], axis=0)
    cond = jnp.concatenate([c_ctx[None, :], c], axis=0)
    mod = _modulation(cond, w_ada, b_ada).reshape(depth, 1 + b_lat, N_MOD, D_MODEL)
    seqs = [(0, b_ctx, t_ctx), (n_ctx, b_lat, t_lat)]
    s_zero = jnp.zeros((b_ctx, 2, N_HEAD, HEAD, HEAD), F32)
    final_row = final_g.reshape(1, D_MODEL)

    states = []
    y_final = None
    for l in range(depth):
        lp = _layer_params(l, w_in, mu_shift, w0, w_up, a0, a_up, g_up, k_k, k_a, r_k, lnx_g, lnx_b,
                           w_branch_a, ln_v_g, w_s, b_s, w_branch_b, w_out, w1, w2, norm1_g, norm2_g)
        tm_proj = math.gcd(TM_PROJ, n_ctx, t_lat)
        row_proj = _mod_row_map(n_ctx, t_lat, tm_proj)
        row_tm = _mod_row_map(n_ctx, t_lat, TM_TOK)
        z_rwkv = _in_proj(x, mod[l], lp["norm1_g"], lp["w_in_rwkv"], 1152, row_proj, tm_proj)
        z_rest = _in_proj(x, mod[l], lp["norm1_g"], lp["w_in_rest"], 1024, row_proj, tm_proj)
        r, v, nk, g, bv, lw, kd, bd = _prep(z_rwkv, lp, n_ctx, t_lat)
        h0 = jnp.concatenate([s_zero, state_rwkv[:, l]], axis=0)
        y0, y1, h_fin = _scan(lw, kd, bd, r, v, nk, h0, seqs)
        states.append(h_fin[:b_ctx])
        ya = _post(y0, y1, bv, g, lp)
        yb = _cmlp(z_rest, lp)
        outs = _ffn(x, ya, yb, z_rest, mod[l], lp, final_row, row_tm, final=(l == depth - 1))
        x = outs[0]
        if l == depth - 1:
            y_final = outs[1]

    y_prompt = y_final[:n_ctx].reshape(b_ctx, t_ctx, D_MODEL)
    y_sample = y_final[n_ctx:].reshape(b_lat, t_lat, D_MODEL)
    return (y_prompt, y_sample, jnp.stack(states, axis=1))
```

```python
import functools
import math

import numpy as np
import jax
import jax.numpy as jnp
from jax import lax
from jax.experimental import pallas as pl
from jax.experimental.pallas import tpu as pltpu

F32 = jnp.float32
BF16 = jnp.bfloat16

D_MODEL = 1024
HEAD = 64
N_HEAD = D_MODEL // HEAD
LORA = 64
G_LORA = 128
C_RWKV = 3 * D_MODEL + 4 * LORA + G_LORA
D_REST = 4 * D_MODEL
D_FF = 4 * D_MODEL
GRID_W = 64
CHUNK = 128
H_B = 8
HEAD_B = D_MODEL // H_B
N_MOD = 6
EPS = 1e-6
GN_EPS = 64e-5
DECAY_SCALE = math.exp(-0.5)

SCAN_L = 64
LANE = 128
SCAN_PASSES = dict(aa=1, x=1, neu=1, wu=1, gs=1, y=1, up=1)
INV_BASE = 8
TM_TOK = 256
TM_PROJ = 2048
VMEM_LIMIT = 56 * 1024 * 1024


def _cparams(sem):
    return pltpu.CompilerParams(dimension_semantics=sem, vmem_limit_bytes=VMEM_LIMIT)


def _split(x):
    hi = x.astype(BF16)
    lo = (x - hi.astype(F32)).astype(BF16)
    return hi, lo


def _dot(a, b, dims=(((1,), (0,)), ((), ()))):
    return lax.dot_general(a, b, dims, preferred_element_type=F32)


_NN = (((1,), (0,)), ((), ()))
_NT = (((1,), (1,)), ((), ()))
_TN = (((0,), (0,)), ((), ()))


def _dot3(a, b, dims=_NN):
    ah, al = _split(a)
    bh, bl = _split(b)
    return _dot(ah, bh, dims) + (_dot(ah, bl, dims) + _dot(al, bh, dims))


def _seg_sum(x, e_ref, et_ref):
    hi, lo = _split(x)
    s = _dot(hi, e_ref[...]) + _dot(lo, e_ref[...])
    shi, slo = _split(s)
    return _dot(shi, et_ref[...]) + _dot(slo, et_ref[...])


def _mod_kernel(c_ref, w_ref, b_ref, o_ref):
    cond = c_ref[...]
    o_ref[0] = _dot(jax.nn.silu(cond).astype(BF16), w_ref[0].astype(BF16)) + b_ref[0]


def _modulation(cond, w_ada, b_ada):
    depth = w_ada.shape[0]
    n = cond.shape[0]
    tn = 1536
    return pl.pallas_call(
        _mod_kernel,
        grid=(depth, (N_MOD * D_MODEL) // tn),
        in_specs=[pl.BlockSpec((n, D_MODEL), lambda l, j: (0, 0)),
                  pl.BlockSpec((1, D_MODEL, tn), lambda l, j: (l, 0, j)),
                  pl.BlockSpec((1, 1, tn), lambda l, j: (l, 0, j))],
        out_specs=pl.BlockSpec((1, n, tn), lambda l, j: (l, 0, j)),
        out_shape=jax.ShapeDtypeStruct((depth, n, N_MOD * D_MODEL), F32),
        compiler_params=_cparams(("parallel", "parallel")),
        name="modulation",
    )(cond, w_ada, b_ada.reshape(depth, 1, N_MOD * D_MODEL))


def _mod_row_map(n_ctx_tok, t_lat, tm):
    def row(i):
        tok = i * tm
        return jnp.where(tok < n_ctx_tok, 0, 1 + (tok - n_ctx_tok) // t_lat)
    return row


def _rms_mod(x, g, shift, scale):
    y = x * lax.rsqrt(jnp.mean(x * x, axis=-1, keepdims=True) + EPS) * g
    return y * (1.0 + scale) + shift


def _in_proj_kernel(x_ref, mod_ref, g_ref, w_ref, o_ref, h_scr):
    @pl.when(pl.program_id(1) == 0)
    def _():
        m = mod_ref[0]
        h_scr[...] = _rms_mod(x_ref[...], g_ref[...], m[0:1], m[1:2]).astype(BF16)

    o_ref[...] = _dot(h_scr[...], w_ref[...])


def _in_proj(x, mod_l, g, w, tn, mod_row, tm):
    n_tok = x.shape[0]
    n_out = w.shape[1]
    return pl.pallas_call(
        _in_proj_kernel,
        grid=(n_tok // tm, n_out // tn),
        in_specs=[pl.BlockSpec((tm, D_MODEL), lambda i, j: (i, 0)),
                  pl.BlockSpec((1, N_MOD, D_MODEL), lambda i, j: (mod_row(i), 0, 0)),
                  pl.BlockSpec((1, D_MODEL), lambda i, j: (0, 0)),
                  pl.BlockSpec((D_MODEL, tn), lambda i, j: (0, j))],
        out_specs=pl.BlockSpec((tm, tn), lambda i, j: (i, j)),
        out_shape=jax.ShapeDtypeStruct((n_tok, n_out), F32),
        scratch_shapes=[pltpu.VMEM((tm, D_MODEL), BF16)],
        compiler_params=_cparams(("parallel", "arbitrary")),
        name="in_proj",
    )(x, mod_l, g, w)


def _shift_rows(z, k, fill_first):
    return jnp.concatenate([fill_first, z[: z.shape[0] - k]], axis=0)


def _prep_kernel(z_ref, zp_ref, zn_ref, mu_ref, wup_ref, aup_ref, gup_ref, w0_ref, a0_ref, kk_ref, ka_ref,
                 rk_ref, e_ref, et_ref,
                 r_ref, v_ref, nk_ref, g_ref, bv_ref, lw_ref, kd_ref, bd_ref, zs_scr,
                 *, n_ctx_tiles, lat_tiles_per_seq):
    i = pl.program_id(0)
    tm = z_ref.shape[0]
    z = z_ref[...]
    row = lax.broadcasted_iota(jnp.int32, z.shape, 0)
    lane = lax.broadcasted_iota(jnp.int32, z.shape, 1)
    prev1 = jnp.where(row == 0, 0.0, pltpu.roll(z, 1, 0))
    next1 = jnp.where(row == tm - 1, 0.0, pltpu.roll(z, tm - 1, 0))

    @pl.when(i < n_ctx_tiles)
    def _():
        zs_scr[...] = jnp.where(lane % 2 == 0, prev1, next1)

    @pl.when(i >= n_ctx_tiles)
    def _():
        j = (i - n_ctx_tiles) % lat_tiles_per_seq
        col = row % GRID_W
        left = jnp.where(col == 0, 0.0, prev1)
        right = jnp.where(col == GRID_W - 1, 0.0, next1)
        up_halo = jnp.where(j == 0, 0.0, zp_ref[...])
        dn_halo = jnp.where(j == lat_tiles_per_seq - 1, 0.0, zn_ref[...])
        up = jnp.concatenate([up_halo, z[: tm - GRID_W]], axis=0)
        down = jnp.concatenate([z[GRID_W:], dn_halo], axis=0)
        m = lane % 4
        zs_scr[...] = jnp.where(m == 0, left, jnp.where(m == 1, right, jnp.where(m == 2, up, down)))

    zs = z + mu_ref[...] * (zs_scr[...] - z)
    d = D_MODEL
    r = zs[:, 0:d]
    k = zs[:, d:2 * d]
    v = zs[:, 2 * d:3 * d]
    wd = zs[:, 3 * d:3 * d + 2 * LORA]
    ad = zs[:, 3 * d + 2 * LORA:3 * d + 4 * LORA]
    gd = zs[:, 3 * d + 4 * LORA:]
    r_ref[...] = r
    v_ref[...] = v
    g_ref[...] = _dot(jax.nn.sigmoid(gd).astype(BF16), gup_ref[...])
    w_logit = w0_ref[...] + _dot(jnp.tanh(wd).astype(BF16), wup_ref[...])
    a_all = jax.nn.sigmoid(a0_ref[...] + _dot(ad.astype(BF16), aup_ref[...]))
    kk = k * kk_ref[...]
    kk = kk * lax.rsqrt(_seg_sum(kk * kk, e_ref, et_ref) + 1e-12)
    nk_ref[...] = -kk
    ka = ka_ref[...]
    kd_sum = None
    for dr in range(2):
        a = a_all[:, dr * d:(dr + 1) * d]
        lw_ref[dr] = -DECAY_SCALE * jax.nn.sigmoid(w_logit[:, dr * d:(dr + 1) * d])
        kd = k * (1.0 + (a - 1.0) * ka)
        kd_ref[dr] = kd
        bd_ref[dr] = a * kk
        kd_sum = kd if kd_sum is None else kd_sum + kd
    bv_ref[...] = _seg_sum(kd_sum * rk_ref[...] * r, e_ref, et_ref) * v


def _prep(z_rwkv, lp, n_ctx_tok, t_lat):
    n_tok = z_rwkv.shape[0]
    tm = TM_TOK
    hb = tm // GRID_W
    n_hblk = n_tok // GRID_W
    tok_spec = pl.BlockSpec((tm, D_MODEL), lambda i: (i, 0))
    dir_spec = pl.BlockSpec((2, tm, D_MODEL), lambda i: (0, i, 0))

    def const(shape):
        return pl.BlockSpec(shape, lambda i: (0,) * len(shape))

    kern = functools.partial(_prep_kernel, n_ctx_tiles=n_ctx_tok // tm, lat_tiles_per_seq=t_lat // tm)
    tok_shape = jax.ShapeDtypeStruct((n_tok, D_MODEL), F32)
    dir_shape = jax.ShapeDtypeStruct((2, n_tok, D_MODEL), F32)
    return pl.pallas_call(
        kern,
        grid=(n_tok // tm,),
        in_specs=[pl.BlockSpec((tm, C_RWKV), lambda i: (i, 0)),
                  pl.BlockSpec((GRID_W, C_RWKV), lambda i: (jnp.maximum(i * hb - 1, 0), 0)),
                  pl.BlockSpec((GRID_W, C_RWKV), lambda i: (jnp.minimum((i + 1) * hb, n_hblk - 1), 0)),
                  const((1, C_RWKV)), const((2 * LORA, 2 * D_MODEL)), const((2 * LORA, 2 * D_MODEL)),
                  const((G_LORA, D_MODEL)), const((1, 2 * D_MODEL)), const((1, 2 * D_MODEL)),
                  const((1, D_MODEL)), const((1, D_MODEL)), const((1, D_MODEL)),
                  const((D_MODEL, LANE)), const((LANE, D_MODEL))],
        out_specs=[tok_spec, tok_spec, tok_spec, tok_spec, tok_spec, dir_spec, dir_spec, dir_spec],
        out_shape=[tok_shape] * 5 + [dir_shape] * 3,
        scratch_shapes=[pltpu.VMEM((tm, C_RWKV), F32)],
        compiler_params=_cparams(("parallel",)),
        name="rwkv_prep",
    )(z_rwkv, z_rwkv, z_rwkv, lp["mu"], lp["wup"], lp["aup"], lp["gup"], lp["w0"], lp["a0"], lp["k_k"],
      lp["k_a"], lp["r_k"], lp["e"], lp["et"])


def _mm(a, b, dims=_NN, passes=3):
    out = _dot(a[0], b[0], dims)
    if passes == 3:
        out = out + (_dot(a[0], b[1], dims) + _dot(a[1], b[0], dims))
    return out


def _cols(p, sl):
    return (p[0][:, sl], p[1][:, sl])


def _rows(p, sl):
    return (p[0][sl], p[1][sl])


def _cat(ps, axis):
    return (jnp.concatenate([p[0] for p in ps], axis=axis), jnp.concatenate([p[1] for p in ps], axis=axis))


def _scan_kernel(blk_ref, first_ref, seq_ref, *refs):
    del blk_ref, seq_ref
    in_refs = (refs[0:6], refs[6:12])
    h0_ref, y_refs, hT_ref, s_scr = refs[12], refs[13:15], refs[15], refs[16]
    s = pl.program_id(0)
    n = in_refs[0][0].shape[1]
    heads = range(in_refs[0][0].shape[2] // HEAD)
    hs = [slice(h * HEAD, (h + 1) * HEAD) for h in heads]
    lo_half = slice(0, n)
    hi_half = slice(n, 2 * n)

    @pl.when(first_ref[s] == 1)
    def _():
        s_scr[...] = h0_ref[0]

    row = lax.broadcasted_iota(jnp.int32, (n, n), 0)
    col = lax.broadcasted_iota(jnp.int32, (n, n), 1)
    row2 = lax.broadcasted_iota(jnp.int32, (n, 2 * n), 0)
    lane2 = lax.broadcasted_iota(jnp.int32, (n, 2 * n), 1)
    left = lane2 < n
    col2 = jnp.where(left, lane2, lane2 - n)
    eye_right = jnp.where(jnp.logical_and(jnp.logical_not(left), row2 == col2), 1.0, 0.0)

    incl2, strict2, nr, bk, bkh, vs, v, g_tot = [], [], [], [], [], [], [], []
    for dr in range(2):
        lw_ref, kd_ref, bd_ref, r_ref, v_ref, nk_ref = in_refs[dr]
        sign = 1 - 2 * dr
        tri = ((row - col) * sign >= 0).astype(BF16)
        incl2.append((row2 - col2) * sign >= 0)
        strict2.append((row2 - col2) * sign > 0)
        lw = lw_ref[0]
        lhi, llo = _split(lw)
        cum = _dot(tri, lhi) + _dot(tri, llo)
        tot = cum[n - 1:n, :] if dr == 0 else cum[0:1, :]
        g_inv = jnp.exp(-cum)
        g_rest = jnp.exp(tot - cum)
        g_tot.append(jnp.exp(tot))
        kd = kd_ref[0]
        bd = bd_ref[0]
        v.append(v_ref[...])
        nr.append(_split(jnp.concatenate([nk_ref[...] * jnp.exp(cum - lw), r_ref[...] * jnp.exp(cum)], axis=0)))
        bk.append(_split(jnp.concatenate([bd * g_inv, kd * g_inv], axis=0)))
        bkh.append(_split(jnp.concatenate([bd * g_rest, kd * g_rest], axis=0)))
        vs.append(_split(v[dr]))

    units = [(dr, h) for h in heads for dr in range(2)]
    ps = SCAN_PASSES
    aa = [_mm(_cols(nr[d], hs[h]), _cols(bk[d], hs[h]), _NT, ps["aa"]) for d, h in units]
    top = [jnp.where(strict2[d], a[:n], 0.0) for a, (d, h) in zip(aa, units)]
    bot = [_split(jnp.where(incl2[d], a[n:], 0.0)) for a, (d, h) in zip(aa, units)]
    tops = [_split(t) for t in top]
    x = [_mm(_cols(tp, hi_half), _cols(vs[d], hs[h]), _NN, ps["x"]) for tp, (d, h) in zip(tops, units)]
    shift = INV_BASE.bit_length() - 1
    same_base = jnp.right_shift(row2, shift) == jnp.right_shift(col2, shift)
    slab = [jnp.where(left, jnp.where(same_base, t, 0.0), eye_right) for t in top]
    m = 1
    while m < INV_BASE:
        sp = [_split(sb) for sb in slab]
        slab = [_mm(_cols(p, lo_half), p, _NN, ps["neu"]) + jnp.where(left, 0.0, sb) for p, sb in zip(sp, slab)]
        m *= 2
    tinv = [sb[:, hi_half] for sb in slab]
    a_sq = [t[:, lo_half] for t in top]
    b = INV_BASE
    while b < n:
        sb_, s2b = b.bit_length() - 1, b.bit_length()
        off = jnp.logical_and(jnp.right_shift(row, s2b) == jnp.right_shift(col, s2b),
                              jnp.right_shift(row, sb_) != jnp.right_shift(col, sb_))
        ts = [_split(t) for t in tinv]
        z = [_mm(_split(jnp.where(off, a, 0.0)), t, _NN, ps["neu"]) for a, t in zip(a_sq, ts)]
        tinv = [t + _mm(tp, _split(zz), _NN, ps["neu"]) for t, tp, zz in zip(tinv, ts, z)]
        b *= 2
    st = [s_scr[d, h] for d, h in units]
    gs = [_mm(_cols(nr[d], hs[h]), _split(sh), _NT, ps["gs"]) for (d, h), sh in zip(units, st)]
    u = [_mm(_split(t), _split(g[:n] + xh), _NN, ps["wu"]) for t, g, xh in zip(tinv, gs, x)]
    uv = [_split(jnp.concatenate([uh, v[d][:, hs[h]]], axis=0)) for uh, (d, h) in zip(u, units)]
    for i, (d, h) in enumerate(units):
        y_refs[d][:, hs[h]] = gs[i][n:] + _mm(bot[i], uv[i], _NN, ps["y"])
    for i, (d, h) in enumerate(units):
        s_scr[d, h] = st[i] * g_tot[d][:, hs[h]] + _mm(uv[i], _cols(bkh[d], hs[h]), _TN, ps["up"])

    hT_ref[0] = s_scr[...]


def _scan_tables(seqs):
    blk = [[], []]
    first, seq = [], []
    sid = 0
    for tok0, nb, t in seqs:
        nc = t // SCAN_L
        for b in range(nb):
            base = (tok0 + b * t) // SCAN_L
            for c in range(nc):
                blk[0].append(base + c)
                blk[1].append(base + nc - 1 - c)
                first.append(1 if c == 0 else 0)
                seq.append(sid)
            sid += 1
    return (jnp.asarray(np.array(blk, np.int32).reshape(-1)), jnp.asarray(np.array(first, np.int32)),
            jnp.asarray(np.array(seq, np.int32)), len(first))


def _scan(lw, kd, bd, r, v, nk, h0, seqs):
    assert SCAN_L == HEAD
    n_tok = r.shape[0]
    n_seq = h0.shape[0]
    blk, first, seq, n_steps = _scan_tables(seqs)

    def dir_spec(d):
        return pl.BlockSpec((1, SCAN_L, D_MODEL), lambda s, blk, first, seq: (d, blk[d * n_steps + s], 0))

    def tok_spec(d):
        return pl.BlockSpec((SCAN_L, D_MODEL), lambda s, blk, first, seq: (blk[d * n_steps + s], 0))

    st_spec = pl.BlockSpec((1, 2, N_HEAD, HEAD, HEAD), lambda s, blk, first, seq: (seq[s], 0, 0, 0, 0))
    in_specs, args = [], []
    for d in range(2):
        in_specs += [dir_spec(d)] * 3 + [tok_spec(d)] * 3
        args += [lw, kd, bd, r, v, nk]
    tok_shape = jax.ShapeDtypeStruct((n_tok, D_MODEL), F32)
    return pl.pallas_call(
        _scan_kernel,
        grid_spec=pltpu.PrefetchScalarGridSpec(
            num_scalar_prefetch=3,
            grid=(n_steps,),
            in_specs=in_specs + [st_spec],
            out_specs=[tok_spec(0), tok_spec(1), st_spec],
            scratch_shapes=[pltpu.VMEM((2, N_HEAD, HEAD, HEAD), F32)]),
        out_shape=[tok_shape, tok_shape, jax.ShapeDtypeStruct((n_seq, 2, N_HEAD, HEAD, HEAD), F32)],
        compiler_params=_cparams(("arbitrary",)),
        name="rwkv_scan",
    )(blk, first, seq, *args, h0)


def _post_kernel(y0_ref, y1_ref, bv_ref, g_ref, lg_ref, lb_ref, e_ref, et_ref, w_ref, o_ref):
    y = y0_ref[...] + y1_ref[...]
    mu = _seg_sum(y, e_ref, et_ref) * (1.0 / HEAD)
    yc = y - mu
    var = _seg_sum(yc * yc, e_ref, et_ref) * (1.0 / HEAD)
    yn = yc * lax.rsqrt(var + GN_EPS) * lg_ref[...] + lb_ref[...]
    o_ref[...] = _dot(((yn + bv_ref[...]) * g_ref[...]).astype(BF16), w_ref[...])


def _post(y0, y1, bv, g, lp):
    n_tok = bv.shape[0]
    tm = TM_TOK
    tok_spec = pl.BlockSpec((tm, D_MODEL), lambda i: (i, 0))

    def const(shape):
        return pl.BlockSpec(shape, lambda i: (0,) * len(shape))

    return pl.pallas_call(
        _post_kernel,
        grid=(n_tok // tm,),
        in_specs=[tok_spec, tok_spec, tok_spec, tok_spec, const((1, D_MODEL)), const((1, D_MODEL)),
                  const((D_MODEL, LANE)), const((LANE, D_MODEL)), const((D_MODEL, D_MODEL))],
        out_specs=tok_spec,
        out_shape=jax.ShapeDtypeStruct((n_tok, D_MODEL), F32),
        compiler_params=_cparams(("parallel",)),
        name="rwkv_post",
    )(y0, y1, bv, g, lp["lnx_g"], lp["lnx_b"], lp["e"], lp["et"], lp["w_branch_a"])


def _cmlp_kernel(zu_ref, zv_ref, lg_ref, ws_ref, bs_ref, w_ref, o_ref, y_scr):
    v = jax.nn.gelu(zv_ref[...])
    mu = jnp.mean(v, axis=-1, keepdims=True)
    vc = v - mu
    var = jnp.mean(vc * vc, axis=-1, keepdims=True)
    vn = (vc * lax.rsqrt(var + EPS) * lg_ref[...]).astype(BF16)
    u = jax.nn.gelu(zu_ref[...])
    for c in range(zu_ref.shape[0] // CHUNK):
        rows = slice(c * CHUNK, (c + 1) * CHUNK)
        for h in range(H_B):
            cols = slice(h * HEAD_B, (h + 1) * HEAD_B)
            s = _dot(ws_ref[h], vn[rows, cols]) + bs_ref[:, cols]
            y_scr[rows, cols] = (u[rows, cols] * s).astype(BF16)
    o_ref[...] = _dot(y_scr[...], w_ref[...])


def _cmlp(z_rest, lp):
    n_tok = z_rest.shape[0]
    tm = TM_TOK

    def const(shape):
        return pl.BlockSpec(shape, lambda i: (0,) * len(shape))

    return pl.pallas_call(
        _cmlp_kernel,
        grid=(n_tok // tm,),
        in_specs=[pl.BlockSpec((tm, D_MODEL), lambda i: (i, 0)),
                  pl.BlockSpec((tm, D_MODEL), lambda i: (i, 1)),
                  const((1, D_MODEL)), const((H_B, CHUNK, CHUNK)), const((CHUNK, D_MODEL)),
                  const((D_MODEL, D_MODEL))],
        out_specs=pl.BlockSpec((tm, D_MODEL), lambda i: (i, 0)),
        out_shape=jax.ShapeDtypeStruct((n_tok, D_MODEL), F32),
        scratch_shapes=[pltpu.VMEM((tm, D_MODEL), BF16)],
        compiler_params=_cparams(("parallel",)),
        name="chunk_mlp",
    )(z_rest, z_rest, lp["ln_v_g"], lp["w_s"], lp["b_s"], lp["w_branch_b"])


def _ffn_kernel(x_ref, ya_ref, yb_ref, ga_ref, gb_ref, mod_ref, g2_ref, wo_ref, w1_ref, w2_ref, fg_ref,
                o_ref, *maybe_final, final):
    m = mod_ref[0]
    mixed = jax.nn.sigmoid(ga_ref[...]) * ya_ref[...] + jax.nn.sigmoid(gb_ref[...]) * yb_ref[...]
    x = x_ref[...] + m[2:3] * _dot(mixed.astype(BF16), wo_ref[...])
    h2 = _rms_mod(x, g2_ref[...], m[3:4], m[4:5]).astype(BF16)
    acc = jnp.zeros(x.shape, F32)
    ff_chunk = D_MODEL
    for c in range(D_FF // ff_chunk):
        cols = slice(c * ff_chunk, (c + 1) * ff_chunk)
        hid = jnp.square(jnp.maximum(_dot(h2, w1_ref[:, cols]), 0.0)).astype(BF16)
        acc = acc + _dot(hid, w2_ref[cols, :])
    x = x + m[5:6] * acc
    o_ref[...] = x
    if final:
        yf = x * lax.rsqrt(jnp.mean(x * x, axis=-1, keepdims=True) + EPS) * fg_ref[...]
        maybe_final[0][...] = yf


def _ffn(x, ya, yb, z_rest, mod_l, lp, final_g, mod_row, final):
    n_tok = x.shape[0]
    tm = TM_TOK
    tok_spec = pl.BlockSpec((tm, D_MODEL), lambda i: (i, 0))

    def const(shape):
        return pl.BlockSpec(shape, lambda i: (0,) * len(shape), pipeline_mode=pl.Buffered(1))

    tok_shape = jax.ShapeDtypeStruct((n_tok, D_MODEL), F32)
    return pl.pallas_call(
        functools.partial(_ffn_kernel, final=final),
        grid=(n_tok // tm,),
        in_specs=[tok_spec, tok_spec, tok_spec,
                  pl.BlockSpec((tm, D_MODEL), lambda i: (i, 2)),
                  pl.BlockSpec((tm, D_MODEL), lambda i: (i, 3)),
                  pl.BlockSpec((1, N_MOD, D_MODEL), lambda i: (mod_row(i), 0, 0)),
                  const((1, D_MODEL)), const((D_MODEL, D_MODEL)), const((D_MODEL, D_FF)),
                  const((D_FF, D_MODEL)), const((1, D_MODEL))],
        out_specs=[tok_spec, tok_spec] if final else [tok_spec],
        out_shape=[tok_shape, tok_shape] if final else [tok_shape],
        compiler_params=_cparams(("parallel",)),
        name="mix_ffn",
    )(x, ya, yb, z_rest, z_rest, mod_l, lp["norm2_g"], lp["w_out"], lp["w1"], lp["w2"], final_g)


def _block_diag2(m):
    z = jnp.zeros_like(m[0])
    return jnp.concatenate([jnp.concatenate([m[0], z], axis=1), jnp.concatenate([z, m[1]], axis=1)], axis=0)


def _layer_params(l, w_in, mu_shift, w0, w_up, a0, a_up, g_up, k_k, k_a, r_k, lnx_g, lnx_b, w_branch_a,
                  ln_v_g, w_s, b_s, w_branch_b, w_out, w1, w2, norm1_g, norm2_g):
    head_of = np.arange(D_MODEL) // HEAD
    e = (head_of[:, None] == np.arange(LANE)[None, :]).astype(np.float32)
    row = lambda a: a.reshape(1, -1)
    return dict(
        w_in_rwkv=w_in[l][:, :C_RWKV].astype(BF16), w_in_rest=w_in[l][:, C_RWKV:].astype(BF16),
        mu=row(mu_shift[l]), wup=_block_diag2(w_up[l]).astype(BF16), aup=_block_diag2(a_up[l]).astype(BF16),
        gup=g_up[l].astype(BF16), w0=row(w0[l]), a0=row(a0[l]), k_k=row(k_k[l]), k_a=row(k_a[l]),
        r_k=row(r_k[l]), lnx_g=row(lnx_g[l]), lnx_b=row(lnx_b[l]), w_branch_a=w_branch_a[l].astype(BF16),
        ln_v_g=row(ln_v_g[l]), w_s=w_s[l].astype(BF16), b_s=jnp.repeat(b_s[l].T, HEAD_B, axis=1),
        w_branch_b=w_branch_b[l].astype(BF16), w_out=w_out[l].astype(BF16), w1=w1[l].astype(BF16),
        w2=w2[l].astype(BF16), norm1_g=row(norm1_g[l]), norm2_g=row(norm2_g[l]),
        e=jnp.asarray(e, BF16), et=jnp.asarray(e.T, BF16))


def kernel(x_prompt, x_sample, state_rwkv, c, c_ctx, w_ada, b_ada, norm1_g, norm2_g, w_in, mu_shift, w0, w_up,
           a0, a_up, g_up, k_k, k_a, r_k, lnx_g, lnx_b, w_branch_a, ln_v_g, w_s, b_s, w_branch_b, w_out, w1, w2,
           final_g):
    b_ctx, t_ctx, _ = x_prompt.shape
    b_lat, t_lat, _ = x_sample.shape
    depth = w_in.shape[0]
    n_ctx = b_ctx * t_ctx
    n_lat = b_lat * t_lat
    assert t_ctx == TM_TOK and t_lat % TM_TOK == 0 and t_lat % GRID_W == 0

    x = jnp.concatenate([x_prompt.reshape(n_ctx, D_MODEL), x_sample.reshape(n_lat, D_MODEL)], axis=0)
    cond = jnp.concatenate([c_ctx[None, :], c], axis=0)
    mod = _modulation(cond, w_ada, b_ada).reshape(depth, 1 + b_lat, N_MOD, D_MODEL)
    seqs = [(0, b_ctx, t_ctx), (n_ctx, b_lat, t_lat)]
    s_zero = jnp.zeros((b_ctx, 2, N_HEAD, HEAD, HEAD), F32)
    final_row = final_g.reshape(1, D_MODEL)

    states = []
    y_final = None
    for l in range(depth):
        lp = _layer_params(l, w_in, mu_shift, w0, w_up, a0, a_up, g_up, k_k, k_a, r_k, lnx_g, lnx_b,
                           w_branch_a, ln_v_g, w_s, b_s, w_branch_b, w_out, w1, w2, norm1_g, norm2_g)
        tm_proj = math.gcd(TM_PROJ, n_ctx, t_lat)
        row_proj = _mod_row_map(n_ctx, t_lat, tm_proj)
        row_tm = _mod_row_map(n_ctx, t_lat, TM_TOK)
        z_rwkv = _in_proj(x, mod[l], lp["norm1_g"], lp["w_in_rwkv"], 1152, row_proj, tm_proj)
        z_rest = _in_proj(x, mod[l], lp["norm1_g"], lp["w_in_rest"], 1024, row_proj, tm_proj)
        r, v, nk, g, bv, lw, kd, bd = _prep(z_rwkv, lp, n_ctx, t_lat)
        h0 = jnp.concatenate([s_zero, state_rwkv[:, l]], axis=0)
        y0, y1, h_fin = _scan(lw, kd, bd, r, v, nk, h0, seqs)
        states.append(h_fin[:b_ctx])
        ya = _post(y0, y1, bv, g, lp)
        yb = _cmlp(z_rest, lp)
        outs = _ffn(x, ya, yb, z_rest, mod[l], lp, final_row, row_tm, final=(l == depth - 1))
        x = outs[0]
        if l == depth - 1:
            y_final = outs[1]

    y_prompt = y_final[:n_ctx].reshape(b_ctx, t_ctx, D_MODEL)
    y_sample = y_final[n_ctx:].reshape(b_lat, t_lat, D_MODEL)
    return (y_prompt, y_sample, jnp.stack(states, axis=1))
```

```python
import functools
import math

import numpy as np
import jax
import jax.numpy as jnp
from jax import lax
from jax.experimental import pallas as pl
from jax.experimental.pallas import tpu as pltpu

F32 = jnp.float32
BF16 = jnp.bfloat16

D_MODEL = 1024
HEAD = 64
N_HEAD = D_MODEL // HEAD
LORA = 64
G_LORA = 128
C_RWKV = 3 * D_MODEL + 4 * LORA + G_LORA
D_REST = 4 * D_MODEL
D_FF = 4 * D_MODEL
GRID_W = 64
CHUNK = 128
H_B = 8
HEAD_B = D_MODEL // H_B
N_MOD = 6
EPS = 1e-6
GN_EPS = 64e-5
DECAY_SCALE = math.exp(-0.5)

SCAN_L = 64
LANE = 128
SCAN_PASSES = dict(aa=1, x=1, neu=1, wu=1, gs=1, y=1, up=1)
INV_BASE = 8
TM_TOK = 256
TM_PROJ = 2048
VMEM_LIMIT = 56 * 1024 * 1024


def _cparams(sem):
    return pltpu.CompilerParams(dimension_semantics=sem, vmem_limit_bytes=VMEM_LIMIT)


def _split(x):
    hi = x.astype(BF16)
    lo = (x - hi.astype(F32)).astype(BF16)
    return hi, lo


def _dot(a, b, dims=(((1,), (0,)), ((), ()))):
    return lax.dot_general(a, b, dims, preferred_element_type=F32)


_NN = (((1,), (0,)), ((), ()))
_NT = (((1,), (1,)), ((), ()))
_TN = (((0,), (0,)), ((), ()))


def _dot3(a, b, dims=_NN):
    ah, al = _split(a)
    bh, bl = _split(b)
    return _dot(ah, bh, dims) + (_dot(ah, bl, dims) + _dot(al, bh, dims))


def _seg_sum(x, e_ref, et_ref):
    hi, lo = _split(x)
    s = _dot(hi, e_ref[...]) + _dot(lo, e_ref[...])
    shi, slo = _split(s)
    return _dot(shi, et_ref[...]) + _dot(slo, et_ref[...])


def _mod_kernel(c_ref, w_ref, b_ref, o_ref):
    cond = c_ref[...]
    o_ref[0] = _dot(jax.nn.silu(cond).astype(BF16), w_ref[0].astype(BF16)) + b_ref[0]


def _modulation(cond, w_ada, b_ada):
    depth = w_ada.shape[0]
    n = cond.shape[0]
    tn = 1536
    return pl.pallas_call(
        _mod_kernel,
        grid=(depth, (N_MOD * D_MODEL) // tn),
        in_specs=[pl.BlockSpec((n, D_MODEL), lambda l, j: (0, 0)),
                  pl.BlockSpec((1, D_MODEL, tn), lambda l, j: (l, 0, j)),
                  pl.BlockSpec((1, 1, tn), lambda l, j: (l, 0, j))],
        out_specs=pl.BlockSpec((1, n, tn), lambda l, j: (l, 0, j)),
        out_shape=jax.ShapeDtypeStruct((depth, n, N_MOD * D_MODEL), F32),
        compiler_params=_cparams(("parallel", "parallel")),
        name="modulation",
    )(cond, w_ada, b_ada.reshape(depth, 1, N_MOD * D_MODEL))


def _mod_row_map(n_ctx_tok, t_lat, tm):
    def row(i):
        tok = i * tm
        return jnp.where(tok < n_ctx_tok, 0, 1 + (tok - n_ctx_tok) // t_lat)
    return row


def _rms_mod(x, g, shift, scale):
    y = x * lax.rsqrt(jnp.mean(x * x, axis=-1, keepdims=True) + EPS) * g
    return y * (1.0 + scale) + shift


def _in_proj_kernel(x_ref, mod_ref, g_ref, w_ref, o_ref, h_scr):
    @pl.when(pl.program_id(1) == 0)
    def _():
        m = mod_ref[0]
        h_scr[...] = _rms_mod(x_ref[...], g_ref[...], m[0:1], m[1:2]).astype(BF16)

    o_ref[...] = _dot(h_scr[...], w_ref[...])


def _in_proj(x, mod_l, g, w, tn, mod_row, tm):
    n_tok = x.shape[0]
    n_out = w.shape[1]
    return pl.pallas_call(
        _in_proj_kernel,
        grid=(n_tok // tm, n_out // tn),
        in_specs=[pl.BlockSpec((tm, D_MODEL), lambda i, j: (i, 0)),
                  pl.BlockSpec((1, N_MOD, D_MODEL), lambda i, j: (mod_row(i), 0, 0)),
                  pl.BlockSpec((1, D_MODEL), lambda i, j: (0, 0)),
                  pl.BlockSpec((D_MODEL, tn), lambda i, j: (0, j))],
        out_specs=pl.BlockSpec((tm, tn), lambda i, j: (i, j)),
        out_shape=jax.ShapeDtypeStruct((n_tok, n_out), F32),
        scratch_shapes=[pltpu.VMEM((tm, D_MODEL), BF16)],
        compiler_params=_cparams(("parallel", "arbitrary")),
        name="in_proj",
    )(x, mod_l, g, w)


def _shift_rows(z, k, fill_first):
    return jnp.concatenate([fill_first, z[: z.shape[0] - k]], axis=0)


def _prep_kernel(z_ref, zp_ref, zn_ref, mu_ref, wup_ref, aup_ref, gup_ref, w0_ref, a0_ref, kk_ref, ka_ref,
                 rk_ref, e_ref, et_ref,
                 r_ref, v_ref, nk_ref, g_ref, bv_ref, lw_ref, kd_ref, bd_ref, zs_scr,
                 *, n_ctx_tiles, lat_tiles_per_seq):
    i = pl.program_id(0)
    tm = z_ref.shape[0]
    z = z_ref[...]
    row = lax.broadcasted_iota(jnp.int32, z.shape, 0)
    lane = lax.broadcasted_iota(jnp.int32, z.shape, 1)
    prev1 = jnp.where(row == 0, 0.0, pltpu.roll(z, 1, 0))
    next1 = jnp.where(row == tm - 1, 0.0, pltpu.roll(z, tm - 1, 0))

    @pl.when(i < n_ctx_tiles)
    def _():
        zs_scr[...] = jnp.where(lane % 2 == 0, prev1, next1)

    @pl.when(i >= n_ctx_tiles)
    def _():
        j = (i - n_ctx_tiles) % lat_tiles_per_seq
        col = row % GRID_W
        left = jnp.where(col == 0, 0.0, prev1)
        right = jnp.where(col == GRID_W - 1, 0.0, next1)
        up_halo = jnp.where(j == 0, 0.0, zp_ref[...])
        dn_halo = jnp.where(j == lat_tiles_per_seq - 1, 0.0, zn_ref[...])
        up = jnp.concatenate([up_halo, z[: tm - GRID_W]], axis=0)
        down = jnp.concatenate([z[GRID_W:], dn_halo], axis=0)
        m = lane % 4
        zs_scr[...] = jnp.where(m == 0, left, jnp.where(m == 1, right, jnp.where(m == 2, up, down)))

    zs = z + mu_ref[...] * (zs_scr[...] - z)
    d = D_MODEL
    r = zs[:, 0:d]
    k = zs[:, d:2 * d]
    v = zs[:, 2 * d:3 * d]
    wd = zs[:, 3 * d:3 * d + 2 * LORA]
    ad = zs[:, 3 * d + 2 * LORA:3 * d + 4 * LORA]
    gd = zs[:, 3 * d + 4 * LORA:]
    r_ref[...] = r
    v_ref[...] = v
    g_ref[...] = _dot(jax.nn.sigmoid(gd).astype(BF16), gup_ref[...])
    w_logit = w0_ref[...] + _dot(jnp.tanh(wd).astype(BF16), wup_ref[...])
    a_all = jax.nn.sigmoid(a0_ref[...] + _dot(ad.astype(BF16), aup_ref[...]))
    kk = k * kk_ref[...]
    kk = kk * lax.rsqrt(_seg_sum(kk * kk, e_ref, et_ref) + 1e-12)
    nk_ref[...] = -kk
    ka = ka_ref[...]
    kd_sum = None
    for dr in range(2):
        a = a_all[:, dr * d:(dr + 1) * d]
        lw_ref[dr] = -DECAY_SCALE * jax.nn.sigmoid(w_logit[:, dr * d:(dr + 1) * d])
        kd = k * (1.0 + (a - 1.0) * ka)
        kd_ref[dr] = kd
        bd_ref[dr] = a * kk
        kd_sum = kd if kd_sum is None else kd_sum + kd
    bv_ref[...] = _seg_sum(kd_sum * rk_ref[...] * r, e_ref, et_ref) * v


def _prep(z_rwkv, lp, n_ctx_tok, t_lat):
    n_tok = z_rwkv.shape[0]
    tm = TM_TOK
    hb = tm // GRID_W
    n_hblk = n_tok // GRID_W
    tok_spec = pl.BlockSpec((tm, D_MODEL), lambda i: (i, 0))
    dir_spec = pl.BlockSpec((2, tm, D_MODEL), lambda i: (0, i, 0))

    def const(shape):
        return pl.BlockSpec(shape, lambda i: (0,) * len(shape))

    kern = functools.partial(_prep_kernel, n_ctx_tiles=n_ctx_tok // tm, lat_tiles_per_seq=t_lat // tm)
    tok_shape = jax.ShapeDtypeStruct((n_tok, D_MODEL), F32)
    dir_shape = jax.ShapeDtypeStruct((2, n_tok, D_MODEL), F32)
    return pl.pallas_call(
        kern,
        grid=(n_tok // tm,),
        in_specs=[pl.BlockSpec((tm, C_RWKV), lambda i: (i, 0)),
                  pl.BlockSpec((GRID_W, C_RWKV), lambda i: (jnp.maximum(i * hb - 1, 0), 0)),
                  pl.BlockSpec((GRID_W, C_RWKV), lambda i: (jnp.minimum((i + 1) * hb, n_hblk - 1), 0)),
                  const((1, C_RWKV)), const((2 * LORA, 2 * D_MODEL)), const((2 * LORA, 2 * D_MODEL)),
                  const((G_LORA, D_MODEL)), const((1, 2 * D_MODEL)), const((1, 2 * D_MODEL)),
                  const((1, D_MODEL)), const((1, D_MODEL)), const((1, D_MODEL)),
                  const((D_MODEL, LANE)), const((LANE, D_MODEL))],
        out_specs=[tok_spec, tok_spec, tok_spec, tok_spec, tok_spec, dir_spec, dir_spec, dir_spec],
        out_shape=[tok_shape] * 5 + [dir_shape] * 3,
        scratch_shapes=[pltpu.VMEM((tm, C_RWKV), F32)],
        compiler_params=_cparams(("parallel",)),
        name="rwkv_prep",
    )(z_rwkv, z_rwkv, z_rwkv, lp["mu"], lp["wup"], lp["aup"], lp["gup"], lp["w0"], lp["a0"], lp["k_k"],
      lp["k_a"], lp["r_k"], lp["e"], lp["et"])


def _mm(a, b, dims=_NN, passes=3):
    out = _dot(a[0], b[0], dims)
    if passes == 3:
        out = out + (_dot(a[0], b[1], dims) + _dot(a[1], b[0], dims))
    return out


def _cols(p, sl):
    return (p[0][:, sl], p[1][:, sl])


def _rows(p, sl):
    return (p[0][sl], p[1][sl])


def _cat(ps, axis):
    return (jnp.concatenate([p[0] for p in ps], axis=axis), jnp.concatenate([p[1] for p in ps], axis=axis))


def _scan_kernel(blk_ref, first_ref, seq_ref, *refs):
    del blk_ref, seq_ref
    in_refs = (refs[0:6], refs[6:12])
    h0_ref, y_refs, hT_ref, s_scr = refs[12], refs[13:15], refs[15], refs[16]
    s = pl.program_id(0)
    n = in_refs[0][0].shape[1]
    heads = range(in_refs[0][0].shape[2] // HEAD)
    hs = [slice(h * HEAD, (h + 1) * HEAD) for h in heads]
    lo_half = slice(0, n)
    hi_half = slice(n, 2 * n)

    @pl.when(first_ref[s] == 1)
    def _():
        s_scr[...] = h0_ref[0]

    @pl.when(first_ref[s] == 2)
    def _():
        s_scr[...] = jnp.zeros(s_scr.shape, F32)

    row = lax.broadcasted_iota(jnp.int32, (n, n), 0)
    col = lax.broadcasted_iota(jnp.int32, (n, n), 1)
    row2 = lax.broadcasted_iota(jnp.int32, (n, 2 * n), 0)
    lane2 = lax.broadcasted_iota(jnp.int32, (n, 2 * n), 1)
    left = lane2 < n
    col2 = jnp.where(left, lane2, lane2 - n)
    eye_right = jnp.where(jnp.logical_and(jnp.logical_not(left), row2 == col2), 1.0, 0.0)

    incl2, strict2, nr, bk, bkh, vs, v, g_tot = [], [], [], [], [], [], [], []
    for dr in range(2):
        lw_ref, kd_ref, bd_ref, r_ref, v_ref, nk_ref = in_refs[dr]
        sign = 1 - 2 * dr
        tri = ((row - col) * sign >= 0).astype(BF16)
        incl2.append((row2 - col2) * sign >= 0)
        strict2.append((row2 - col2) * sign > 0)
        lw = lw_ref[0]
        lhi, llo = _split(lw)
        cum = _dot(tri, lhi) + _dot(tri, llo)
        tot = cum[n - 1:n, :] if dr == 0 else cum[0:1, :]
        g_inv = jnp.exp(-cum)
        g_rest = jnp.exp(tot - cum)
        g_tot.append(jnp.exp(tot))
        kd = kd_ref[0]
        bd = bd_ref[0]
        v.append(v_ref[...])
        nr.append(_split(jnp.concatenate([nk_ref[...] * jnp.exp(cum - lw), r_ref[...] * jnp.exp(cum)], axis=0)))
        bk.append(_split(jnp.concatenate([bd * g_inv, kd * g_inv], axis=0)))
        bkh.append(_split(jnp.concatenate([bd * g_rest, kd * g_rest], axis=0)))
        vs.append(_split(v[dr]))

    units = [(dr, h) for h in heads for dr in range(2)]
    ps = SCAN_PASSES
    aa = [_mm(_cols(nr[d], hs[h]), _cols(bk[d], hs[h]), _NT, ps["aa"]) for d, h in units]
    top = [jnp.where(strict2[d], a[:n], 0.0) for a, (d, h) in zip(aa, units)]
    bot = [_split(jnp.where(incl2[d], a[n:], 0.0)) for a, (d, h) in zip(aa, units)]
    tops = [_split(t) for t in top]
    x = [_mm(_cols(tp, hi_half), _cols(vs[d], hs[h]), _NN, ps["x"]) for tp, (d, h) in zip(tops, units)]
    shift = INV_BASE.bit_length() - 1
    same_base = jnp.right_shift(row2, shift) == jnp.right_shift(col2, shift)
    slab = [jnp.where(left, jnp.where(same_base, t, 0.0), eye_right) for t in top]
    m = 1
    while m < INV_BASE:
        sp = [_split(sb) for sb in slab]
        slab = [_mm(_cols(p, lo_half), p, _NN, ps["neu"]) + jnp.where(left, 0.0, sb) for p, sb in zip(sp, slab)]
        m *= 2
    tinv = [sb[:, hi_half] for sb in slab]
    a_sq = [t[:, lo_half] for t in top]
    b = INV_BASE
    while b < n:
        sb_, s2b = b.bit_length() - 1, b.bit_length()
        off = jnp.logical_and(jnp.right_shift(row, s2b) == jnp.right_shift(col, s2b),
                              jnp.right_shift(row, sb_) != jnp.right_shift(col, sb_))
        ts = [_split(t) for t in tinv]
        z = [_mm(_split(jnp.where(off, a, 0.0)), t, _NN, ps["neu"]) for a, t in zip(a_sq, ts)]
        tinv = [t + _mm(tp, _split(zz), _NN, ps["neu"]) for t, tp, zz in zip(tinv, ts, z)]
        b *= 2
    st = [s_scr[d, h] for d, h in units]
    gs = [_mm(_cols(nr[d], hs[h]), _split(sh), _NT, ps["gs"]) for (d, h), sh in zip(units, st)]
    u = [_mm(_split(t), _split(g[:n] + xh), _NN, ps["wu"]) for t, g, xh in zip(tinv, gs, x)]
    uv = [_split(jnp.concatenate([uh, v[d][:, hs[h]]], axis=0)) for uh, (d, h) in zip(u, units)]
    for i, (d, h) in enumerate(units):
        y_refs[d][:, hs[h]] = gs[i][n:] + _mm(bot[i], uv[i], _NN, ps["y"])
    for i, (d, h) in enumerate(units):
        s_scr[d, h] = st[i] * g_tot[d][:, hs[h]] + _mm(uv[i], _cols(bkh[d], hs[h]), _TN, ps["up"])

    hT_ref[0] = s_scr[...]


def _scan_tables(seqs):
    blk = [[], []]
    first, seq = [], []
    sid = 0
    for tok0, nb, t, zero_state in seqs:
        nc = t // SCAN_L
        for b in range(nb):
            base = (tok0 + b * t) // SCAN_L
            for c in range(nc):
                blk[0].append(base + c)
                blk[1].append(base + nc - 1 - c)
                first.append((2 if zero_state else 1) if c == 0 else 0)
                seq.append(sid)
            sid += 1
    return (jnp.asarray(np.array(blk, np.int32).reshape(-1)), jnp.asarray(np.array(first, np.int32)),
            jnp.asarray(np.array(seq, np.int32)), len(first))


def _scan(lw, kd, bd, r, v, nk, h0, seqs):
    assert SCAN_L == HEAD
    n_tok = r.shape[0]
    n_zero = sum(nb for _, nb, _, zero_state in seqs if zero_state)
    n_seq = n_zero + h0.shape[0]
    assert all(zero_state for _, _, _, zero_state in seqs[:1]) and n_seq == sum(nb for _, nb, _, _ in seqs)
    blk, first, seq, n_steps = _scan_tables(seqs)

    def dir_spec(d):
        return pl.BlockSpec((1, SCAN_L, D_MODEL), lambda s, blk, first, seq: (d, blk[d * n_steps + s], 0))

    def tok_spec(d):
        return pl.BlockSpec((SCAN_L, D_MODEL), lambda s, blk, first, seq: (blk[d * n_steps + s], 0))

    st_spec = pl.BlockSpec((1, 2, N_HEAD, HEAD, HEAD), lambda s, blk, first, seq: (seq[s], 0, 0, 0, 0))
    h0_spec = pl.BlockSpec((1, 2, N_HEAD, HEAD, HEAD),
                           lambda s, blk, first, seq: (jnp.maximum(seq[s] - n_zero, 0), 0, 0, 0, 0))
    in_specs, args = [], []
    for d in range(2):
        in_specs += [dir_spec(d)] * 3 + [tok_spec(d)] * 3
        args += [lw, kd, bd, r, v, nk]
    tok_shape = jax.ShapeDtypeStruct((n_tok, D_MODEL), F32)
    return pl.pallas_call(
        _scan_kernel,
        grid_spec=pltpu.PrefetchScalarGridSpec(
            num_scalar_prefetch=3,
            grid=(n_steps,),
            in_specs=in_specs + [h0_spec],
            out_specs=[tok_spec(0), tok_spec(1), st_spec],
            scratch_shapes=[pltpu.VMEM((2, N_HEAD, HEAD, HEAD), F32)]),
        out_shape=[tok_shape, tok_shape, jax.ShapeDtypeStruct((n_seq, 2, N_HEAD, HEAD, HEAD), F32)],
        compiler_params=_cparams(("arbitrary",)),
        name="rwkv_scan",
    )(blk, first, seq, *args, h0)


def _post_kernel(y0_ref, y1_ref, bv_ref, g_ref, lg_ref, lb_ref, e_ref, et_ref, w_ref, o_ref):
    y = y0_ref[...] + y1_ref[...]
    mu = _seg_sum(y, e_ref, et_ref) * (1.0 / HEAD)
    yc = y - mu
    var = _seg_sum(yc * yc, e_ref, et_ref) * (1.0 / HEAD)
    yn = yc * lax.rsqrt(var + GN_EPS) * lg_ref[...] + lb_ref[...]
    o_ref[...] = _dot(((yn + bv_ref[...]) * g_ref[...]).astype(BF16), w_ref[...])


def _post(y0, y1, bv, g, lp):
    n_tok = bv.shape[0]
    tm = TM_TOK
    tok_spec = pl.BlockSpec((tm, D_MODEL), lambda i: (i, 0))

    def const(shape):
        return pl.BlockSpec(shape, lambda i: (0,) * len(shape))

    return pl.pallas_call(
        _post_kernel,
        grid=(n_tok // tm,),
        in_specs=[tok_spec, tok_spec, tok_spec, tok_spec, const((1, D_MODEL)), const((1, D_MODEL)),
                  const((D_MODEL, LANE)), const((LANE, D_MODEL)), const((D_MODEL, D_MODEL))],
        out_specs=tok_spec,
        out_shape=jax.ShapeDtypeStruct((n_tok, D_MODEL), F32),
        compiler_params=_cparams(("parallel",)),
        name="rwkv_post",
    )(y0, y1, bv, g, lp["lnx_g"], lp["lnx_b"], lp["e"], lp["et"], lp["w_branch_a"])


def _cmlp_kernel(zu_ref, zv_ref, lg_ref, ws_ref, bs_ref, w_ref, o_ref, y_scr):
    v = jax.nn.gelu(zv_ref[...])
    mu = jnp.mean(v, axis=-1, keepdims=True)
    vc = v - mu
    var = jnp.mean(vc * vc, axis=-1, keepdims=True)
    vn = (vc * lax.rsqrt(var + EPS) * lg_ref[...]).astype(BF16)
    u = jax.nn.gelu(zu_ref[...])
    for c in range(zu_ref.shape[0] // CHUNK):
        rows = slice(c * CHUNK, (c + 1) * CHUNK)
        for h in range(H_B):
            cols = slice(h * HEAD_B, (h + 1) * HEAD_B)
            s = _dot(ws_ref[h], vn[rows, cols]) + bs_ref[:, cols]
            y_scr[rows, cols] = (u[rows, cols] * s).astype(BF16)
    o_ref[...] = _dot(y_scr[...], w_ref[...])


def _cmlp(z_rest, lp):
    n_tok = z_rest.shape[0]
    tm = TM_TOK

    def const(shape):
        return pl.BlockSpec(shape, lambda i: (0,) * len(shape))

    return pl.pallas_call(
        _cmlp_kernel,
        grid=(n_tok // tm,),
        in_specs=[pl.BlockSpec((tm, D_MODEL), lambda i: (i, 0)),
                  pl.BlockSpec((tm, D_MODEL), lambda i: (i, 1)),
                  const((1, D_MODEL)), const((H_B, CHUNK, CHUNK)), const((CHUNK, D_MODEL)),
                  const((D_MODEL, D_MODEL))],
        out_specs=pl.BlockSpec((tm, D_MODEL), lambda i: (i, 0)),
        out_shape=jax.ShapeDtypeStruct((n_tok, D_MODEL), F32),
        scratch_shapes=[pltpu.VMEM((tm, D_MODEL), BF16)],
        compiler_params=_cparams(("parallel",)),
        name="chunk_mlp",
    )(z_rest, z_rest, lp["ln_v_g"], lp["w_s"], lp["b_s"], lp["w_branch_b"])


def _ffn_kernel(x_ref, ya_ref, yb_ref, ga_ref, gb_ref, mod_ref, g2_ref, wo_ref, w1_ref, w2_ref, fg_ref,
                o_ref, *maybe_final, final):
    m = mod_ref[0]
    mixed = jax.nn.sigmoid(ga_ref[...]) * ya_ref[...] + jax.nn.sigmoid(gb_ref[...]) * yb_ref[...]
    x = x_ref[...] + m[2:3] * _dot(mixed.astype(BF16), wo_ref[...])
    h2 = _rms_mod(x, g2_ref[...], m[3:4], m[4:5]).astype(BF16)
    acc = jnp.zeros(x.shape, F32)
    ff_chunk = D_MODEL
    for c in range(D_FF // ff_chunk):
        cols = slice(c * ff_chunk, (c + 1) * ff_chunk)
        hid = jnp.square(jnp.maximum(_dot(h2, w1_ref[:, cols]), 0.0)).astype(BF16)
        acc = acc + _dot(hid, w2_ref[cols, :])
    x = x + m[5:6] * acc
    o_ref[...] = x
    if final:
        yf = x * lax.rsqrt(jnp.mean(x * x, axis=-1, keepdims=True) + EPS) * fg_ref[...]
        maybe_final[0][...] = yf


def _ffn(x, ya, yb, z_rest, mod_l, lp, final_g, mod_row, final):
    n_tok = x.shape[0]
    tm = TM_TOK
    tok_spec = pl.BlockSpec((tm, D_MODEL), lambda i: (i, 0))

    def const(shape):
        return pl.BlockSpec(shape, lambda i: (0,) * len(shape), pipeline_mode=pl.Buffered(1))

    tok_shape = jax.ShapeDtypeStruct((n_tok, D_MODEL), F32)
    return pl.pallas_call(
        functools.partial(_ffn_kernel, final=final),
        grid=(n_tok // tm,),
        in_specs=[tok_spec, tok_spec, tok_spec,
                  pl.BlockSpec((tm, D_MODEL), lambda i: (i, 2)),
                  pl.BlockSpec((tm, D_MODEL), lambda i: (i, 3)),
                  pl.BlockSpec((1, N_MOD, D_MODEL), lambda i: (mod_row(i), 0, 0)),
                  const((1, D_MODEL)), const((D_MODEL, D_MODEL)), const((D_MODEL, D_FF)),
                  const((D_FF, D_MODEL)), const((1, D_MODEL))],
        out_specs=[tok_spec, tok_spec] if final else [tok_spec],
        out_shape=[tok_shape, tok_shape] if final else [tok_shape],
        compiler_params=_cparams(("parallel",)),
        name="mix_ffn",
    )(x, ya, yb, z_rest, z_rest, mod_l, lp["norm2_g"], lp["w_out"], lp["w1"], lp["w2"], final_g)


def _block_diag2(m):
    z = jnp.zeros_like(m[0])
    return jnp.concatenate([jnp.concatenate([m[0], z], axis=1), jnp.concatenate([z, m[1]], axis=1)], axis=0)


def _layer_params(l, w_in, mu_shift, w0, w_up, a0, a_up, g_up, k_k, k_a, r_k, lnx_g, lnx_b, w_branch_a,
                  ln_v_g, w_s, b_s, w_branch_b, w_out, w1, w2, norm1_g, norm2_g):
    head_of = np.arange(D_MODEL) // HEAD
    e = (head_of[:, None] == np.arange(LANE)[None, :]).astype(np.float32)
    row = lambda a: a.reshape(1, -1)
    return dict(
        w_in_rwkv=w_in[l][:, :C_RWKV].astype(BF16), w_in_rest=w_in[l][:, C_RWKV:].astype(BF16),
        mu=row(mu_shift[l]), wup=_block_diag2(w_up[l]).astype(BF16), aup=_block_diag2(a_up[l]).astype(BF16),
        gup=g_up[l].astype(BF16), w0=row(w0[l]), a0=row(a0[l]), k_k=row(k_k[l]), k_a=row(k_a[l]),
        r_k=row(r_k[l]), lnx_g=row(lnx_g[l]), lnx_b=row(lnx_b[l]), w_branch_a=w_branch_a[l].astype(BF16),
        ln_v_g=row(ln_v_g[l]), w_s=w_s[l].astype(BF16), b_s=jnp.repeat(b_s[l].T, HEAD_B, axis=1),
        w_branch_b=w_branch_b[l].astype(BF16), w_out=w_out[l].astype(BF16), w1=w1[l].astype(BF16),
        w2=w2[l].astype(BF16), norm1_g=row(norm1_g[l]), norm2_g=row(norm2_g[l]),
        e=jnp.asarray(e, BF16), et=jnp.asarray(e.T, BF16))


def kernel(x_prompt, x_sample, state_rwkv, c, c_ctx, w_ada, b_ada, norm1_g, norm2_g, w_in, mu_shift, w0, w_up,
           a0, a_up, g_up, k_k, k_a, r_k, lnx_g, lnx_b, w_branch_a, ln_v_g, w_s, b_s, w_branch_b, w_out, w1, w2,
           final_g):
    b_ctx, t_ctx, _ = x_prompt.shape
    b_lat, t_lat, _ = x_sample.shape
    depth = w_in.shape[0]
    n_ctx = b_ctx * t_ctx
    n_lat = b_lat * t_lat
    assert t_ctx == TM_TOK and t_lat % TM_TOK == 0 and t_lat % GRID_W == 0

    x = jnp.concatenate([x_prompt.reshape(n_ctx, D_MODEL), x_sample.reshape(n_lat, D_MODEL)], axis=0)
    cond = jnp.concatenate([c_ctx[None, :], c], axis=0)
    mod = _modulation(cond, w_ada, b_ada).reshape(depth, 1 + b_lat, N_MOD, D_MODEL)
    seqs = [(0, b_ctx, t_ctx, True), (n_ctx, b_lat, t_lat, False)]
    final_row = final_g.reshape(1, D_MODEL)

    states = []
    y_final = None
    for l in range(depth):
        lp = _layer_params(l, w_in, mu_shift, w0, w_up, a0, a_up, g_up, k_k, k_a, r_k, lnx_g, lnx_b,
                           w_branch_a, ln_v_g, w_s, b_s, w_branch_b, w_out, w1, w2, norm1_g, norm2_g)
        tm_proj = math.gcd(TM_PROJ, n_ctx, t_lat)
        row_proj = _mod_row_map(n_ctx, t_lat, tm_proj)
        row_tm = _mod_row_map(n_ctx, t_lat, TM_TOK)
        z_rwkv = _in_proj(x, mod[l], lp["norm1_g"], lp["w_in_rwkv"], 1152, row_proj, tm_proj)
        z_rest = _in_proj(x, mod[l], lp["norm1_g"], lp["w_in_rest"], 1024, row_proj, tm_proj)
        r, v, nk, g, bv, lw, kd, bd = _prep(z_rwkv, lp, n_ctx, t_lat)
        y0, y1, h_fin = _scan(lw, kd, bd, r, v, nk, state_rwkv[:, l], seqs)
        states.append(h_fin[:b_ctx])
        ya = _post(y0, y1, bv, g, lp)
        yb = _cmlp(z_rest, lp)
        outs = _ffn(x, ya, yb, z_rest, mod[l], lp, final_row, row_tm, final=(l == depth - 1))
        x = outs[0]
        if l == depth - 1:
            y_final = outs[1]

    y_prompt = y_final[:n_ctx].reshape(b_ctx, t_ctx, D_MODEL)
    y_sample = y_final[n_ctx:].reshape(b_lat, t_lat, D_MODEL)
    return (y_prompt, y_sample, jnp.stack(states, axis=1))
```

```python
import functools
import math

import numpy as np
import jax
import jax.numpy as jnp
from jax import lax
from jax.experimental import pallas as pl
from jax.experimental.pallas import tpu as pltpu

F32 = jnp.float32
BF16 = jnp.bfloat16

D_MODEL = 1024
HEAD = 64
N_HEAD = D_MODEL // HEAD
LORA = 64
G_LORA = 128
C_RWKV = 3 * D_MODEL + 4 * LORA + G_LORA
D_REST = 4 * D_MODEL
D_FF = 4 * D_MODEL
GRID_W = 64
CHUNK = 128
H_B = 8
HEAD_B = D_MODEL // H_B
N_MOD = 6
EPS = 1e-6
GN_EPS = 64e-5
DECAY_SCALE = math.exp(-0.5)

SCAN_L = 64
LANE = 128
SCAN_PASSES = dict(aa=1, x=1, neu=1, wu=1, gs=1, y=1, up=1)
INV_BASE = 8
TM_TOK = 256
TM_PROJ = 2048
VMEM_LIMIT = 56 * 1024 * 1024


def _cparams(sem):
    return pltpu.CompilerParams(dimension_semantics=sem, vmem_limit_bytes=VMEM_LIMIT)


def _split(x):
    hi = x.astype(BF16)
    lo = (x - hi.astype(F32)).astype(BF16)
    return hi, lo


def _dot(a, b, dims=(((1,), (0,)), ((), ()))):
    return lax.dot_general(a, b, dims, preferred_element_type=F32)


_NN = (((1,), (0,)), ((), ()))
_NT = (((1,), (1,)), ((), ()))
_TN = (((0,), (0,)), ((), ()))


def _dot3(a, b, dims=_NN):
    ah, al = _split(a)
    bh, bl = _split(b)
    return _dot(ah, bh, dims) + (_dot(ah, bl, dims) + _dot(al, bh, dims))


def _seg_sum(x, e_ref, et_ref):
    hi, lo = _split(x)
    s = _dot(hi, e_ref[...]) + _dot(lo, e_ref[...])
    shi, slo = _split(s)
    return _dot(shi, et_ref[...]) + _dot(slo, et_ref[...])


def _mod_kernel(c_ref, w_ref, b_ref, o_ref):
    cond = c_ref[...]
    o_ref[0] = _dot(jax.nn.silu(cond).astype(BF16), w_ref[0].astype(BF16)) + b_ref[0]


def _modulation(cond, w_ada, b_ada):
    depth = w_ada.shape[0]
    n = cond.shape[0]
    tn = 1536
    return pl.pallas_call(
        _mod_kernel,
        grid=(depth, (N_MOD * D_MODEL) // tn),
        in_specs=[pl.BlockSpec((n, D_MODEL), lambda l, j: (0, 0)),
                  pl.BlockSpec((1, D_MODEL, tn), lambda l, j: (l, 0, j)),
                  pl.BlockSpec((1, 1, tn), lambda l, j: (l, 0, j))],
        out_specs=pl.BlockSpec((1, n, tn), lambda l, j: (l, 0, j)),
        out_shape=jax.ShapeDtypeStruct((depth, n, N_MOD * D_MODEL), F32),
        compiler_params=_cparams(("parallel", "parallel")),
        name="modulation",
    )(cond, w_ada, b_ada.reshape(depth, 1, N_MOD * D_MODEL))


def _mod_row_map(n_ctx_tok, t_lat, tm):
    def row(i):
        tok = i * tm
        return jnp.where(tok < n_ctx_tok, 0, 1 + (tok - n_ctx_tok) // t_lat)
    return row


def _rms_mod(x, g, shift, scale):
    y = x * lax.rsqrt(jnp.mean(x * x, axis=-1, keepdims=True) + EPS) * g
    return y * (1.0 + scale) + shift


def _in_proj_kernel(x_ref, mod_ref, g_ref, w_ref, o_ref, h_scr):
    @pl.when(pl.program_id(1) == 0)
    def _():
        m = mod_ref[0]
        h_scr[...] = _rms_mod(x_ref[...], g_ref[...], m[0:1], m[1:2]).astype(BF16)

    o_ref[...] = _dot(h_scr[...], w_ref[...])


def _in_proj(x, mod_l, g, w, tn, mod_row, tm):
    n_tok = x.shape[0]
    n_out = w.shape[1]
    return pl.pallas_call(
        _in_proj_kernel,
        grid=(n_tok // tm, n_out // tn),
        in_specs=[pl.BlockSpec((tm, D_MODEL), lambda i, j: (i, 0)),
                  pl.BlockSpec((1, N_MOD, D_MODEL), lambda i, j: (mod_row(i), 0, 0)),
                  pl.BlockSpec((1, D_MODEL), lambda i, j: (0, 0)),
                  pl.BlockSpec((D_MODEL, tn), lambda i, j: (0, j))],
        out_specs=pl.BlockSpec((tm, tn), lambda i, j: (i, j)),
        out_shape=jax.ShapeDtypeStruct((n_tok, n_out), F32),
        scratch_shapes=[pltpu.VMEM((tm, D_MODEL), BF16)],
        compiler_params=_cparams(("parallel", "arbitrary")),
        name="in_proj",
    )(x, mod_l, g, w)


def _shift_rows(z, k, fill_first):
    return jnp.concatenate([fill_first, z[: z.shape[0] - k]], axis=0)


def _prep_kernel(z_ref, zp_ref, zn_ref, mu_ref, wup_ref, aup_ref, gup_ref, w0_ref, a0_ref, kk_ref, ka_ref,
                 rk_ref, e_ref, et_ref,
                 r_ref, v_ref, nk_ref, g_ref, bv_ref, lw_ref, kd_ref, bd_ref, zs_scr,
                 *, n_ctx_tiles, lat_tiles_per_seq):
    i = pl.program_id(0)
    tm = z_ref.shape[0]
    z = z_ref[...]
    row = lax.broadcasted_iota(jnp.int32, z.shape, 0)
    lane = lax.broadcasted_iota(jnp.int32, z.shape, 1)
    prev1 = jnp.where(row == 0, 0.0, pltpu.roll(z, 1, 0))
    next1 = jnp.where(row == tm - 1, 0.0, pltpu.roll(z, tm - 1, 0))

    @pl.when(i < n_ctx_tiles)
    def _():
        zs_scr[...] = jnp.where(lane % 2 == 0, prev1, next1)

    @pl.when(i >= n_ctx_tiles)
    def _():
        j = (i - n_ctx_tiles) % lat_tiles_per_seq
        col = row % GRID_W
        left = jnp.where(col == 0, 0.0, prev1)
        right = jnp.where(col == GRID_W - 1, 0.0, next1)
        up_halo = jnp.where(j == 0, 0.0, zp_ref[...])
        dn_halo = jnp.where(j == lat_tiles_per_seq - 1, 0.0, zn_ref[...])
        up = jnp.concatenate([up_halo, z[: tm - GRID_W]], axis=0)
        down = jnp.concatenate([z[GRID_W:], dn_halo], axis=0)
        m = lane % 4
        zs_scr[...] = jnp.where(m == 0, left, jnp.where(m == 1, right, jnp.where(m == 2, up, down)))

    zs = z + mu_ref[...] * (zs_scr[...] - z)
    d = D_MODEL
    r = zs[:, 0:d]
    k = zs[:, d:2 * d]
    v = zs[:, 2 * d:3 * d]
    wd = zs[:, 3 * d:3 * d + 2 * LORA]
    ad = zs[:, 3 * d + 2 * LORA:3 * d + 4 * LORA]
    gd = zs[:, 3 * d + 4 * LORA:]
    r_ref[...] = r
    v_ref[...] = v
    g_ref[...] = _dot(jax.nn.sigmoid(gd).astype(BF16), gup_ref[...])
    w_logit = w0_ref[...] + _dot(jnp.tanh(wd).astype(BF16), wup_ref[...])
    a_all = jax.nn.sigmoid(a0_ref[...] + _dot(ad.astype(BF16), aup_ref[...]))
    kk = k * kk_ref[...]
    kk = kk * lax.rsqrt(_seg_sum(kk * kk, e_ref, et_ref) + 1e-12)
    nk_ref[...] = -kk
    ka = ka_ref[...]
    kd_sum = None
    for dr in range(2):
        a = a_all[:, dr * d:(dr + 1) * d]
        lw_ref[dr] = -DECAY_SCALE * jax.nn.sigmoid(w_logit[:, dr * d:(dr + 1) * d])
        kd = k * (1.0 + (a - 1.0) * ka)
        kd_ref[dr] = kd
        bd_ref[dr] = a * kk
        kd_sum = kd if kd_sum is None else kd_sum + kd
    bv_ref[...] = _seg_sum(kd_sum * rk_ref[...] * r, e_ref, et_ref) * v


def _prep(z_rwkv, lp, n_ctx_tok, t_lat):
    n_tok = z_rwkv.shape[0]
    tm = TM_TOK
    hb = tm // GRID_W
    n_hblk = n_tok // GRID_W
    tok_spec = pl.BlockSpec((tm, D_MODEL), lambda i: (i, 0))
    dir_spec = pl.BlockSpec((2, tm, D_MODEL), lambda i: (0, i, 0))

    def const(shape):
        return pl.BlockSpec(shape, lambda i: (0,) * len(shape))

    kern = functools.partial(_prep_kernel, n_ctx_tiles=n_ctx_tok // tm, lat_tiles_per_seq=t_lat // tm)
    tok_shape = jax.ShapeDtypeStruct((n_tok, D_MODEL), F32)
    dir_shape = jax.ShapeDtypeStruct((2, n_tok, D_MODEL), F32)
    return pl.pallas_call(
        kern,
        grid=(n_tok // tm,),
        in_specs=[pl.BlockSpec((tm, C_RWKV), lambda i: (i, 0)),
                  pl.BlockSpec((GRID_W, C_RWKV), lambda i: (jnp.maximum(i * hb - 1, 0), 0)),
                  pl.BlockSpec((GRID_W, C_RWKV), lambda i: (jnp.minimum((i + 1) * hb, n_hblk - 1), 0)),
                  const((1, C_RWKV)), const((2 * LORA, 2 * D_MODEL)), const((2 * LORA, 2 * D_MODEL)),
                  const((G_LORA, D_MODEL)), const((1, 2 * D_MODEL)), const((1, 2 * D_MODEL)),
                  const((1, D_MODEL)), const((1, D_MODEL)), const((1, D_MODEL)),
                  const((D_MODEL, LANE)), const((LANE, D_MODEL))],
        out_specs=[tok_spec, tok_spec, tok_spec, tok_spec, tok_spec, dir_spec, dir_spec, dir_spec],
        out_shape=[tok_shape] * 5 + [dir_shape] * 3,
        scratch_shapes=[pltpu.VMEM((tm, C_RWKV), F32)],
        compiler_params=_cparams(("parallel",)),
        name="rwkv_prep",
    )(z_rwkv, z_rwkv, z_rwkv, lp["mu"], lp["wup"], lp["aup"], lp["gup"], lp["w0"], lp["a0"], lp["k_k"],
      lp["k_a"], lp["r_k"], lp["e"], lp["et"])


def _mm(a, b, dims=_NN, passes=3):
    out = _dot(a[0], b[0], dims)
    if passes == 3:
        out = out + (_dot(a[0], b[1], dims) + _dot(a[1], b[0], dims))
    return out


def _cols(p, sl):
    return (p[0][:, sl], p[1][:, sl])


def _rows(p, sl):
    return (p[0][sl], p[1][sl])


def _cat(ps, axis):
    return (jnp.concatenate([p[0] for p in ps], axis=axis), jnp.concatenate([p[1] for p in ps], axis=axis))


def _scan_kernel(blk_ref, first_ref, seq_ref, *refs):
    del blk_ref, seq_ref
    in_refs = (refs[0:6], refs[6:12])
    h0_ref, y_refs, hT_ref, s_scr = refs[12], refs[13:15], refs[15], refs[16]
    s = pl.program_id(0)
    n = in_refs[0][0].shape[1]
    heads = range(in_refs[0][0].shape[2] // HEAD)
    hs = [slice(h * HEAD, (h + 1) * HEAD) for h in heads]
    lo_half = slice(0, n)
    hi_half = slice(n, 2 * n)

    @pl.when(first_ref[s] == 1)
    def _():
        s_scr[...] = h0_ref[0]

    row = lax.broadcasted_iota(jnp.int32, (n, n), 0)
    col = lax.broadcasted_iota(jnp.int32, (n, n), 1)
    row2 = lax.broadcasted_iota(jnp.int32, (n, 2 * n), 0)
    lane2 = lax.broadcasted_iota(jnp.int32, (n, 2 * n), 1)
    left = lane2 < n
    col2 = jnp.where(left, lane2, lane2 - n)
    eye_right = jnp.where(jnp.logical_and(jnp.logical_not(left), row2 == col2), 1.0, 0.0)

    incl2, strict2, nr, bk, bkh, vs, v, g_tot = [], [], [], [], [], [], [], []
    for dr in range(2):
        lw_ref, kd_ref, bd_ref, r_ref, v_ref, nk_ref = in_refs[dr]
        sign = 1 - 2 * dr
        tri = ((row - col) * sign >= 0).astype(BF16)
        incl2.append((row2 - col2) * sign >= 0)
        strict2.append((row2 - col2) * sign > 0)
        lw = lw_ref[0]
        lhi, llo = _split(lw)
        cum = _dot(tri, lhi) + _dot(tri, llo)
        tot = cum[n - 1:n, :] if dr == 0 else cum[0:1, :]
        g_inv = jnp.exp(-cum)
        g_rest = jnp.exp(tot - cum)
        g_tot.append(jnp.exp(tot))
        kd = kd_ref[0]
        bd = bd_ref[0]
        v.append(v_ref[...])
        nr.append(_split(jnp.concatenate([nk_ref[...] * jnp.exp(cum - lw), r_ref[...] * jnp.exp(cum)], axis=0)))
        bk.append(_split(jnp.concatenate([bd * g_inv, kd * g_inv], axis=0)))
        bkh.append(_split(jnp.concatenate([bd * g_rest, kd * g_rest], axis=0)))
        vs.append(_split(v[dr]))

    units = [(dr, h) for dr in range(2) for h in heads]
    ps = SCAN_PASSES
    aa = [_mm(_cols(nr[d], hs[h]), _cols(bk[d], hs[h]), _NT, ps["aa"]) for d, h in units]
    top = [jnp.where(strict2[d], a[:n], 0.0) for a, (d, h) in zip(aa, units)]
    bot = [_split(jnp.where(incl2[d], a[n:], 0.0)) for a, (d, h) in zip(aa, units)]
    tops = [_split(t) for t in top]
    x = [_mm(_cols(tp, hi_half), _cols(vs[d], hs[h]), _NN, ps["x"]) for tp, (d, h) in zip(tops, units)]
    shift = INV_BASE.bit_length() - 1
    same_base = jnp.right_shift(row2, shift) == jnp.right_shift(col2, shift)
    slab = [jnp.where(left, jnp.where(same_base, t, 0.0), eye_right) for t in top]
    m = 1
    while m < INV_BASE:
        sp = [_split(sb) for sb in slab]
        slab = [_mm(_cols(p, lo_half), p, _NN, ps["neu"]) + jnp.where(left, 0.0, sb) for p, sb in zip(sp, slab)]
        m *= 2
    tinv = [sb[:, hi_half] for sb in slab]
    a_sq = [t[:, lo_half] for t in top]
    b = INV_BASE
    while b < n:
        sb_, s2b = b.bit_length() - 1, b.bit_length()
        off = jnp.logical_and(jnp.right_shift(row, s2b) == jnp.right_shift(col, s2b),
                              jnp.right_shift(row, sb_) != jnp.right_shift(col, sb_))
        ts = [_split(t) for t in tinv]
        z = [_mm(_split(jnp.where(off, a, 0.0)), t, _NN, ps["neu"]) for a, t in zip(a_sq, ts)]
        tinv = [t + _mm(tp, _split(zz), _NN, ps["neu"]) for t, tp, zz in zip(tinv, ts, z)]
        b *= 2
    st = [s_scr[d, h] for d, h in units]
    gs = [_mm(_cols(nr[d], hs[h]), _split(sh), _NT, ps["gs"]) for (d, h), sh in zip(units, st)]
    u = [_mm(_split(t), _split(g[:n] + xh), _NN, ps["wu"]) for t, g, xh in zip(tinv, gs, x)]
    uv = [_split(jnp.concatenate([uh, v[d][:, hs[h]]], axis=0)) for uh, (d, h) in zip(u, units)]
    for i, (d, h) in enumerate(units):
        y_refs[d][:, hs[h]] = gs[i][n:] + _mm(bot[i], uv[i], _NN, ps["y"])
    for i, (d, h) in enumerate(units):
        s_scr[d, h] = st[i] * g_tot[d][:, hs[h]] + _mm(uv[i], _cols(bkh[d], hs[h]), _TN, ps["up"])

    hT_ref[0] = s_scr[...]


def _scan_tables(seqs):
    blk = [[], []]
    first, seq = [], []
    sid = 0
    for tok0, nb, t in seqs:
        nc = t // SCAN_L
        for b in range(nb):
            base = (tok0 + b * t) // SCAN_L
            for c in range(nc):
                blk[0].append(base + c)
                blk[1].append(base + nc - 1 - c)
                first.append(1 if c == 0 else 0)
                seq.append(sid)
            sid += 1
    return (jnp.asarray(np.array(blk, np.int32).reshape(-1)), jnp.asarray(np.array(first, np.int32)),
            jnp.asarray(np.array(seq, np.int32)), len(first))


def _scan(lw, kd, bd, r, v, nk, h0, seqs):
    assert SCAN_L == HEAD
    n_tok = r.shape[0]
    n_seq = h0.shape[0]
    blk, first, seq, n_steps = _scan_tables(seqs)

    def dir_spec(d):
        return pl.BlockSpec((1, SCAN_L, D_MODEL), lambda s, blk, first, seq: (d, blk[d * n_steps + s], 0))

    def tok_spec(d):
        return pl.BlockSpec((SCAN_L, D_MODEL), lambda s, blk, first, seq: (blk[d * n_steps + s], 0))

    st_spec = pl.BlockSpec((1, 2, N_HEAD, HEAD, HEAD), lambda s, blk, first, seq: (seq[s], 0, 0, 0, 0))
    in_specs, args = [], []
    for d in range(2):
        in_specs += [dir_spec(d)] * 3 + [tok_spec(d)] * 3
        args += [lw, kd, bd, r, v, nk]
    tok_shape = jax.ShapeDtypeStruct((n_tok, D_MODEL), F32)
    return pl.pallas_call(
        _scan_kernel,
        grid_spec=pltpu.PrefetchScalarGridSpec(
            num_scalar_prefetch=3,
            grid=(n_steps,),
            in_specs=in_specs + [st_spec],
            out_specs=[tok_spec(0), tok_spec(1), st_spec],
            scratch_shapes=[pltpu.VMEM((2, N_HEAD, HEAD, HEAD), F32)]),
        out_shape=[tok_shape, tok_shape, jax.ShapeDtypeStruct((n_seq, 2, N_HEAD, HEAD, HEAD), F32)],
        compiler_params=_cparams(("arbitrary",)),
        name="rwkv_scan",
    )(blk, first, seq, *args, h0)


def _post_kernel(y0_ref, y1_ref, bv_ref, g_ref, lg_ref, lb_ref, e_ref, et_ref, w_ref, o_ref):
    y = y0_ref[...] + y1_ref[...]
    mu = _seg_sum(y, e_ref, et_ref) * (1.0 / HEAD)
    yc = y - mu
    var = _seg_sum(yc * yc, e_ref, et_ref) * (1.0 / HEAD)
    yn = yc * lax.rsqrt(var + GN_EPS) * lg_ref[...] + lb_ref[...]
    o_ref[...] = _dot(((yn + bv_ref[...]) * g_ref[...]).astype(BF16), w_ref[...])


def _post(y0, y1, bv, g, lp):
    n_tok = bv.shape[0]
    tm = TM_TOK
    tok_spec = pl.BlockSpec((tm, D_MODEL), lambda i: (i, 0))

    def const(shape):
        return pl.BlockSpec(shape, lambda i: (0,) * len(shape))

    return pl.pallas_call(
        _post_kernel,
        grid=(n_tok // tm,),
        in_specs=[tok_spec, tok_spec, tok_spec, tok_spec, const((1, D_MODEL)), const((1, D_MODEL)),
                  const((D_MODEL, LANE)), const((LANE, D_MODEL)), const((D_MODEL, D_MODEL))],
        out_specs=tok_spec,
        out_shape=jax.ShapeDtypeStruct((n_tok, D_MODEL), F32),
        compiler_params=_cparams(("parallel",)),
        name="rwkv_post",
    )(y0, y1, bv, g, lp["lnx_g"], lp["lnx_b"], lp["e"], lp["et"], lp["w_branch_a"])


def _cmlp_kernel(zu_ref, zv_ref, lg_ref, ws_ref, bs_ref, w_ref, o_ref, y_scr):
    v = jax.nn.gelu(zv_ref[...])
    mu = jnp.mean(v, axis=-1, keepdims=True)
    vc = v - mu
    var = jnp.mean(vc * vc, axis=-1, keepdims=True)
    vn = (vc * lax.rsqrt(var + EPS) * lg_ref[...]).astype(BF16)
    u = jax.nn.gelu(zu_ref[...])
    for c in range(zu_ref.shape[0] // CHUNK):
        rows = slice(c * CHUNK, (c + 1) * CHUNK)
        for h in range(H_B):
            cols = slice(h * HEAD_B, (h + 1) * HEAD_B)
            s = _dot(ws_ref[h], vn[rows, cols]) + bs_ref[:, cols]
            y_scr[rows, cols] = (u[rows, cols] * s).astype(BF16)
    o_ref[...] = _dot(y_scr[...], w_ref[...])


def _cmlp(z_rest, lp):
    n_tok = z_rest.shape[0]
    tm = TM_TOK

    def const(shape):
        return pl.BlockSpec(shape, lambda i: (0,) * len(shape))

    return pl.pallas_call(
        _cmlp_kernel,
        grid=(n_tok // tm,),
        in_specs=[pl.BlockSpec((tm, D_MODEL), lambda i: (i, 0)),
                  pl.BlockSpec((tm, D_MODEL), lambda i: (i, 1)),
                  const((1, D_MODEL)), const((H_B, CHUNK, CHUNK)), const((CHUNK, D_MODEL)),
                  const((D_MODEL, D_MODEL))],
        out_specs=pl.BlockSpec((tm, D_MODEL), lambda i: (i, 0)),
        out_shape=jax.ShapeDtypeStruct((n_tok, D_MODEL), F32),
        scratch_shapes=[pltpu.VMEM((tm, D_MODEL), BF16)],
        compiler_params=_cparams(("parallel",)),
        name="chunk_mlp",
    )(z_rest, z_rest, lp["ln_v_g"], lp["w_s"], lp["b_s"], lp["w_branch_b"])


def _ffn_kernel(x_ref, ya_ref, yb_ref, ga_ref, gb_ref, mod_ref, g2_ref, wo_ref, w1_ref, w2_ref, fg_ref,
                o_ref, *maybe_final, final):
    m = mod_ref[0]
    mixed = jax.nn.sigmoid(ga_ref[...]) * ya_ref[...] + jax.nn.sigmoid(gb_ref[...]) * yb_ref[...]
    x = x_ref[...] + m[2:3] * _dot(mixed.astype(BF16), wo_ref[...])
    h2 = _rms_mod(x, g2_ref[...], m[3:4], m[4:5]).astype(BF16)
    acc = jnp.zeros(x.shape, F32)
    ff_chunk = D_MODEL
    for c in range(D_FF // ff_chunk):
        cols = slice(c * ff_chunk, (c + 1) * ff_chunk)
        hid = jnp.square(jnp.maximum(_dot(h2, w1_ref[:, cols]), 0.0)).astype(BF16)
        acc = acc + _dot(hid, w2_ref[cols, :])
    x = x + m[5:6] * acc
    o_ref[...] = x
    if final:
        yf = x * lax.rsqrt(jnp.mean(x * x, axis=-1, keepdims=True) + EPS) * fg_ref[...]
        maybe_final[0][...] = yf


def _ffn(x, ya, yb, z_rest, mod_l, lp, final_g, mod_row, final):
    n_tok = x.shape[0]
    tm = TM_TOK
    tok_spec = pl.BlockSpec((tm, D_MODEL), lambda i: (i, 0))

    def const(shape):
        return pl.BlockSpec(shape, lambda i: (0,) * len(shape), pipeline_mode=pl.Buffered(1))

    tok_shape = jax.ShapeDtypeStruct((n_tok, D_MODEL), F32)
    return pl.pallas_call(
        functools.partial(_ffn_kernel, final=final),
        grid=(n_tok // tm,),
        in_specs=[tok_spec, tok_spec, tok_spec,
                  pl.BlockSpec((tm, D_MODEL), lambda i: (i, 2)),
                  pl.BlockSpec((tm, D_MODEL), lambda i: (i, 3)),
                  pl.BlockSpec((1, N_MOD, D_MODEL), lambda i: (mod_row(i), 0, 0)),
                  const((1, D_MODEL)), const((D_MODEL, D_MODEL)), const((D_MODEL, D_FF)),
                  const((D_FF, D_MODEL)), const((1, D_MODEL))],
        out_specs=[tok_spec, tok_spec] if final else [tok_spec],
        out_shape=[tok_shape, tok_shape] if final else [tok_shape],
        compiler_params=_cparams(("parallel",)),
        name="mix_ffn",
    )(x, ya, yb, z_rest, z_rest, mod_l, lp["norm2_g"], lp["w_out"], lp["w1"], lp["w2"], final_g)


def _block_diag2(m):
    z = jnp.zeros_like(m[0])
    return jnp.concatenate([jnp.concatenate([m[0], z], axis=1), jnp.concatenate([z, m[1]], axis=1)], axis=0)


def _layer_params(l, w_in, mu_shift, w0, w_up, a0, a_up, g_up, k_k, k_a, r_k, lnx_g, lnx_b, w_branch_a,
                  ln_v_g, w_s, b_s, w_branch_b, w_out, w1, w2, norm1_g, norm2_g):
    head_of = np.arange(D_MODEL) // HEAD
    e = (head_of[:, None] == np.arange(LANE)[None, :]).astype(np.float32)
    row = lambda a: a.reshape(1, -1)
    return dict(
        w_in_rwkv=w_in[l][:, :C_RWKV].astype(BF16), w_in_rest=w_in[l][:, C_RWKV:].astype(BF16),
        mu=row(mu_shift[l]), wup=_block_diag2(w_up[l]).astype(BF16), aup=_block_diag2(a_up[l]).astype(BF16),
        gup=g_up[l].astype(BF16), w0=row(w0[l]), a0=row(a0[l]), k_k=row(k_k[l]), k_a=row(k_a[l]),
        r_k=row(r_k[l]), lnx_g=row(lnx_g[l]), lnx_b=row(lnx_b[l]), w_branch_a=w_branch_a[l].astype(BF16),
        ln_v_g=row(ln_v_g[l]), w_s=w_s[l].astype(BF16), b_s=jnp.repeat(b_s[l].T, HEAD_B, axis=1),
        w_branch_b=w_branch_b[l].astype(BF16), w_out=w_out[l].astype(BF16), w1=w1[l].astype(BF16),
        w2=w2[l].astype(BF16), norm1_g=row(norm1_g[l]), norm2_g=row(norm2_g[l]),
        e=jnp.asarray(e, BF16), et=jnp.asarray(e.T, BF16))


def kernel(x_prompt, x_sample, state_rwkv, c, c_ctx, w_ada, b_ada, norm1_g, norm2_g, w_in, mu_shift, w0, w_up,
           a0, a_up, g_up, k_k, k_a, r_k, lnx_g, lnx_b, w_branch_a, ln_v_g, w_s, b_s, w_branch_b, w_out, w1, w2,
           final_g):
    b_ctx, t_ctx, _ = x_prompt.shape
    b_lat, t_lat, _ = x_sample.shape
    depth = w_in.shape[0]
    n_ctx = b_ctx * t_ctx
    n_lat = b_lat * t_lat
    assert t_ctx == TM_TOK and t_lat % TM_TOK == 0 and t_lat % GRID_W == 0

    x = jnp.concatenate([x_prompt.reshape(n_ctx, D_MODEL), x_sample.reshape(n_lat, D_MODEL)], axis=0)
    cond = jnp.concatenate([c_ctx[None, :], c], axis=0)
    mod = _modulation(cond, w_ada, b_ada).reshape(depth, 1 + b_lat, N_MOD, D_MODEL)
    seqs = [(0, b_ctx, t_ctx), (n_ctx, b_lat, t_lat)]
    s_zero = jnp.zeros((b_ctx, 2, N_HEAD, HEAD, HEAD), F32)
    final_row = final_g.reshape(1, D_MODEL)

    states = []
    y_final = None
    for l in range(depth):
        lp = _layer_params(l, w_in, mu_shift, w0, w_up, a0, a_up, g_up, k_k, k_a, r_k, lnx_g, lnx_b,
                           w_branch_a, ln_v_g, w_s, b_s, w_branch_b, w_out, w1, w2, norm1_g, norm2_g)
        tm_proj = math.gcd(TM_PROJ, n_ctx, t_lat)
        row_proj = _mod_row_map(n_ctx, t_lat, tm_proj)
        row_tm = _mod_row_map(n_ctx, t_lat, TM_TOK)
        z_rwkv = _in_proj(x, mod[l], lp["norm1_g"], lp["w_in_rwkv"], 1152, row_proj, tm_proj)
        z_rest = _in_proj(x, mod[l], lp["norm1_g"], lp["w_in_rest"], 1024, row_proj, tm_proj)
        r, v, nk, g, bv, lw, kd, bd = _prep(z_rwkv, lp, n_ctx, t_lat)
        h0 = jnp.concatenate([s_zero, state_rwkv[:, l]], axis=0)
        y0, y1, h_fin = _scan(lw, kd, bd, r, v, nk, h0, seqs)
        states.append(h_fin[:b_ctx])
        ya = _post(y0, y1, bv, g, lp)
        yb = _cmlp(z_rest, lp)
        outs = _ffn(x, ya, yb, z_rest, mod[l], lp, final_row, row_tm, final=(l == depth - 1))
        x = outs[0]
        if l == depth - 1:
            y_final = outs[1]

    y_prompt = y_final[:n_ctx].reshape(b_ctx, t_ctx, D_MODEL)
    y_sample = y_final[n_ctx:].reshape(b_lat, t_lat, D_MODEL)
    return (y_prompt, y_sample, jnp.stack(states, axis=1))
```
